```python
import math
import jax, jax.numpy as jnp
from jax import lax
import numpy as np

D_MODEL = 1024
BATCH = 32
SEQ = 2048
DEPTH = 1

N_META = 16
BLOCK_Q = 128
SB_HEADS = 8
SB_HEAD_DIM = 64
SB_WIDTH = SB_HEADS * SB_HEAD_DIM
DSA_HEADS = 8
DSA_HEAD_DIM = 64
DSA_WIDTH = DSA_HEADS * DSA_HEAD_DIM
IDX_HEADS = 8
IDX_DIM = 64
TOPK_MAX = 256
REL_BUCKETS = 32
REL_MAX_DIST = 128
LN_EPS = 1e-5
DEEPNORM_ALPHA = (2.0 * DEPTH) ** 0.25
DEEPNORM_BETA = (8.0 * DEPTH) ** -0.25
COLS = [SB_WIDTH, SB_WIDTH, SB_WIDTH, SB_WIDTH,
        DSA_WIDTH, DSA_HEAD_DIM, DSA_HEAD_DIM, DSA_WIDTH,
        IDX_HEADS * IDX_DIM, IDX_DIM, IDX_HEADS,
        D_MODEL, D_MODEL]
IN_COLS = sum(COLS)

kernel_name = "hybrid_stickbreak_dsa_gated"


def layer_norm(x, g, b):
    xf = x.astype(jnp.float32)
    mu = jnp.mean(xf, axis=-1, keepdims=True)
    var = jnp.mean(jnp.square(xf - mu), axis=-1, keepdims=True)
    return ((xf - mu) * lax.rsqrt(var + LN_EPS) * g.astype(jnp.float32) + b.astype(jnp.float32)).astype(x.dtype)


def rel_bucket(dist):
    max_exact = REL_BUCKETS // 2
    nf = jnp.maximum(dist, 1).astype(jnp.float32)
    large = max_exact + (jnp.log(nf / max_exact) / math.log(REL_MAX_DIST / max_exact)
                         * (REL_BUCKETS - max_exact)).astype(jnp.int32)
    large = jnp.minimum(large, REL_BUCKETS - 1)
    return jnp.where(dist < max_exact, dist, large)


def stick_breaking_block(q, k, v, q_pos):
    lk = k.shape[1]
    z = jnp.einsum('bqhd,bkhd->bhqk', q, k).astype(jnp.float32) * (SB_HEAD_DIM ** -0.5)
    mask = jnp.arange(lk)[None, :] < q_pos[:, None]
    log_beta = jax.nn.log_sigmoid(z)
    log_1m = jnp.where(mask, log_beta - z, 0.0)
    suffix = lax.cumsum(log_1m, axis=3, reverse=True) - log_1m
    a = jnp.where(mask, jnp.exp(log_beta + suffix), 0.0)
    return jnp.einsum('bhqk,bkhd->bqhd', a.astype(v.dtype), v)


def dsa_block(q, k, v, qi, ki, wi, q_pos, rel_bias, topk):
    lk = k.shape[1]
    s = jnp.einsum('bqhd,bkd->bqhk', qi, ki).astype(jnp.float32)
    score = jnp.einsum('bqhk,bqh->bqk', jax.nn.relu(s), wi.astype(jnp.float32))
    causal = jnp.arange(lk)[None, :] <= q_pos[:, None]
    score = jnp.where(causal[None], score, -jnp.inf)
    kb = min(topk, lk)
    _, idx = lax.top_k(score, kb)
    gather = jax.vmap(lambda a, i: a[i])
    k_sel = gather(k, idx)
    v_sel = gather(v, idx)
    logits = jnp.einsum('bqhd,bqkd->bqhk', q, k_sel).astype(jnp.float32) * (DSA_HEAD_DIM ** -0.5)
    dist = q_pos[None, :, None] - idx
    valid = dist >= 0
    bias = rel_bias[rel_bucket(jnp.maximum(dist, 0))].astype(jnp.float32)
    logits = logits + jnp.transpose(bias, (0, 1, 3, 2))
    logits = jnp.where(valid[:, :, None, :], logits, -1e30)
    p = jax.nn.softmax(logits, axis=-1)
    return jnp.einsum('bqhk,bqkd->bqhd', p.astype(v_sel.dtype), v_sel)


def hybrid_layer(h, w_in, b_gate, idx_kn_g, idx_kn_b, w_pa, w_pb, w_o, ln_g, ln_b, rel_bias, topk):
    bsz, t_len, _ = h.shape
    proj = h @ w_in
    offs = list(np.cumsum(COLS)[:-1])
    (q_a, k_a, v_a, z_a, q_b, k_b, v_b, z_b, qi, ki, wi, ga, gb) = jnp.split(proj, offs, axis=-1)
    q_a = q_a.reshape(bsz, t_len, SB_HEADS, SB_HEAD_DIM)
    k_a = k_a.reshape(bsz, t_len, SB_HEADS, SB_HEAD_DIM)
    v_a = v_a.reshape(bsz, t_len, SB_HEADS, SB_HEAD_DIM)
    q_b = q_b.reshape(bsz, t_len, DSA_HEADS, DSA_HEAD_DIM)
    qi = qi.reshape(bsz, t_len, IDX_HEADS, IDX_DIM)
    ki = layer_norm(ki, idx_kn_g, idx_kn_b)
    wi = wi * (IDX_HEADS ** -0.5 * IDX_DIM ** -0.5)
    gates = jax.nn.sigmoid(jnp.concatenate([ga, gb], axis=-1) + b_gate)
    g_a, g_b = gates[..., :D_MODEL], gates[..., D_MODEL:]

    pos = jnp.arange(t_len, dtype=jnp.int32)
    n_real = t_len - N_META
    bounds = [(0, N_META)] + [(N_META + i * BLOCK_Q, min(N_META + (i + 1) * BLOCK_Q, t_len))
                              for i in range((n_real + BLOCK_Q - 1) // BLOCK_Q)]
    outs_a, outs_b = [], []
    for (s0, e0) in bounds:
        qp = pos[s0:e0]
        outs_a.append(stick_breaking_block(q_a[:, s0:e0], k_a[:, :e0], v_a[:, :e0], qp))
        outs_b.append(dsa_block(q_b[:, s0:e0], k_b[:, :e0], v_b[:, :e0], qi[:, s0:e0],
                                ki[:, :e0], wi[:, s0:e0], qp, rel_bias, topk))
    y_a = jnp.concatenate(outs_a, axis=1).reshape(bsz, t_len, SB_WIDTH) * jax.nn.silu(z_a)
    y_b = jnp.concatenate(outs_b, axis=1).reshape(bsz, t_len, DSA_WIDTH) * jax.nn.silu(z_b)
    merged = g_a * (y_a @ w_pa) + g_b * (y_b @ w_pb)
    out = merged @ w_o
    return layer_norm(DEEPNORM_ALPHA * h + out, ln_g, ln_b)


def setup_inputs(seed: int = 0) -> dict:
    key = jax.random.key(seed)
    ks = jax.random.split(key, 16)
    f32 = jnp.float32
    n = lambda k, shape, s: (jax.random.normal(k, shape, f32) * s).astype(f32)
    return {
        "x": n(ks[0], (BATCH, SEQ, D_MODEL), 1.0),
        "meta_tokens": n(ks[1], (N_META, D_MODEL), 1.0),
        "ln_in_g": 1.0 + n(ks[2], (D_MODEL,), 0.02),
        "ln_in_b": n(ks[3], (D_MODEL,), 0.02),
        "rel_bias": n(ks[4], (REL_BUCKETS, DSA_HEADS), 0.5),
        "w_in": n(ks[5], (DEPTH, D_MODEL, IN_COLS), D_MODEL ** -0.5),
        "b_gate": n(ks[6], (DEPTH, 2 * D_MODEL), 0.02),
        "idx_kn_g": 1.0 + n(ks[7], (DEPTH, IDX_DIM), 0.02),
        "idx_kn_b": n(ks[8], (DEPTH, IDX_DIM), 0.02),
        "w_pa": n(ks[9], (DEPTH, SB_WIDTH, D_MODEL), SB_WIDTH ** -0.5 * DEEPNORM_BETA),
        "w_pb": n(ks[10], (DEPTH, DSA_WIDTH, D_MODEL), DSA_WIDTH ** -0.5 * DEEPNORM_BETA),
        "w_o": n(ks[11], (DEPTH, D_MODEL, D_MODEL), D_MODEL ** -0.5 * DEEPNORM_BETA),
        "ln_g": 1.0 + n(ks[12], (DEPTH, D_MODEL), 0.02),
        "ln_b": n(ks[13], (DEPTH, D_MODEL), 0.02),
    }


def reference(x, meta_tokens, ln_in_g, ln_in_b, rel_bias, w_in, b_gate, idx_kn_g, idx_kn_b,
              w_pa, w_pb, w_o, ln_g, ln_b):
    bsz, seq, d = x.shape
    topk = min(TOPK_MAX, seq // 4)
    meta = jnp.broadcast_to(meta_tokens[None].astype(x.dtype), (bsz, N_META, d))
    h = layer_norm(jnp.concatenate([meta, x], axis=1), ln_in_g, ln_in_b)
    for l in range(DEPTH):
        h = hybrid_layer(h, w_in[l], b_gate[l], idx_kn_g[l], idx_kn_b[l], w_pa[l], w_pb[l],
                         w_o[l], ln_g[l], ln_b[l], rel_bias, topk)
    return h[:, N_META:]
```

```python
import functools
import math

import jax
import jax.numpy as jnp
from jax import lax
from jax.experimental import pallas as pl
from jax.experimental.pallas import tpu as pltpu

D_MODEL = 1024
DEPTH = 1
N_META = 16
HEADS = 8
HEAD_DIM = 64
WIDTH = HEADS * HEAD_DIM
TOPK_MAX = 256
REL_BUCKETS = 32
REL_MAX_DIST = 128
LN_EPS = 1e-5
DEEPNORM_ALPHA = (2.0 * DEPTH) ** 0.25
IDX_SCALE = HEADS ** -0.5 * HEAD_DIM ** -0.5
QK_SCALE = HEAD_DIM ** -0.5

SUBLANES = 8
LANES = 128
QB = 256
KC = 256
PAIRS = HEADS // 2
NEG = -1e30
INT_MIN = -2 ** 31
VMEM_LIMIT = 56 * 1024 * 1024
MXU_DTYPE = jnp.bfloat16

_COLS = [WIDTH, WIDTH, WIDTH, WIDTH, WIDTH, HEAD_DIM, HEAD_DIM, WIDTH, WIDTH, HEAD_DIM, HEADS,
         D_MODEL, D_MODEL]
_OFF = [0]
for _c in _COLS:
    _OFF.append(_OFF[-1] + _c)
(O_QA, O_KA, O_VA, O_ZA, O_QB, O_KB, O_VB, O_ZB, O_QI, O_KI, O_WI, O_GA, O_GB, O_END) = _OFF


def _layer_norm(x, g, b):
    mu = jnp.mean(x, axis=-1, keepdims=True)
    xc = x - mu
    var = jnp.mean(xc * xc, axis=-1, keepdims=True)
    return xc * lax.rsqrt(var + LN_EPS) * g + b


def _dot(a, b):
    return jnp.dot(a, b, preferred_element_type=jnp.float32)


def _dot_nt(a, b):
    return lax.dot_general(a, b, (((1,), (1,)), ((), ())), preferred_element_type=jnp.float32)


W_PROJ_COLS = 6 * WIDTH + 4 * LANES


def _proj_kernel(x_ref, g_ref, b_ref, w_ref, kng_ref, knb_ref,
                 qa_ref, ka_ref, va_ref, qb_ref, qi_ref, ki_ref, kb_ref, vb_ref, wi_ref):
    h = _layer_norm(x_ref[0], g_ref[...], b_ref[...]).astype(MXU_DTYPE)

    def cols(k, width=WIDTH):
        return _dot(h, w_ref[:, k:k + width])

    def store_pairs(ref, val):
        for pr in range(PAIRS):
            ref[0, pr] = val[:, pr * LANES:(pr + 1) * LANES].astype(ref.dtype)

    store_pairs(qa_ref, cols(0) * QK_SCALE)
    store_pairs(ka_ref, cols(WIDTH))
    va_ref[0] = cols(2 * WIDTH).astype(va_ref.dtype)
    store_pairs(qb_ref, cols(3 * WIDTH) * QK_SCALE)
    store_pairs(qi_ref, cols(4 * WIDTH) * cols(5 * WIDTH) * IDX_SCALE)
    base = 6 * WIDTH
    ki_ref[0] = _layer_norm(cols(base, LANES), kng_ref[...], knb_ref[...]).astype(ki_ref.dtype)
    kb_ref[0] = cols(base + LANES, LANES).astype(kb_ref.dtype)
    vb_ref[0] = cols(base + 2 * LANES, LANES).astype(vb_ref.dtype)
    wi_ref[0] = cols(base + 3 * LANES, LANES)


def _proj_call(x, ln_g, ln_b, w_proj, kn_g2, kn_b2, tm):
    bsz, s, d = x.shape
    assert s % tm == 0
    grid = (bsz, s // tm)
    bf = MXU_DTYPE
    pair_shape = jax.ShapeDtypeStruct((bsz, PAIRS, s, LANES), bf)
    pair_spec = pl.BlockSpec((1, PAIRS, tm, LANES), lambda b, i: (b, 0, i, 0))
    row128 = pl.BlockSpec((1, tm, LANES), lambda b, i: (b, i, 0))
    const = lambda shape: pl.BlockSpec(shape, lambda b, i: (0,) * len(shape))
    return pl.pallas_call(
        _proj_kernel,
        grid=grid,
        in_specs=[pl.BlockSpec((1, tm, d), lambda b, i: (b, i, 0)),
                  const((1, d)), const((1, d)), const((d, W_PROJ_COLS)),
                  const((1, LANES)), const((1, LANES))],
        out_specs=[pair_spec, pair_spec,
                   pl.BlockSpec((1, tm, WIDTH), lambda b, i: (b, i, 0)),
                   pair_spec, pair_spec, row128, row128, row128, row128],
        out_shape=[pair_shape, pair_shape, jax.ShapeDtypeStruct((bsz, s, WIDTH), bf),
                   pair_shape, pair_shape,
                   jax.ShapeDtypeStruct((bsz, s, LANES), bf),
                   jax.ShapeDtypeStruct((bsz, s, LANES), bf),
                   jax.ShapeDtypeStruct((bsz, s, LANES), bf),
                   jax.ShapeDtypeStruct((bsz, s, LANES), jnp.float32)],
        compiler_params=pltpu.CompilerParams(
            dimension_semantics=("arbitrary", "arbitrary"), vmem_limit_bytes=VMEM_LIMIT),
        name="proj",
    )(x, ln_g, ln_b, w_proj, kn_g2, kn_b2)


def _group_excl_scan(g_tot, reverse):
    sub = lax.broadcasted_iota(jnp.int32, g_tot.shape, 0)
    out = jnp.zeros_like(g_tot)
    for g in range(SUBLANES):
        row = g_tot[g:g + 1, :]
        take = (sub < g) if reverse else (sub > g)
        out = out + jnp.where(take, row, 0.0)
    return out


def _attn_kernel(qa_ref, qb_ref, qi_ref, wi_ref,
                 ka_ref, vat_ref, ki_ref, kb_ref, vbt_ref,
                 kam_ref, vatm_ref, kim_ref, kbm_ref, vbtm_ref,
                 biasd_ref, biasp_ref, biasm_ref,
                 ya_ref, yb_ref,
                 qpad_scr, key_scr, mb_scr, *, seq, topk):
    i = pl.program_id(1)
    nslab = KC // SUBLANES
    mslab = N_META // SUBLANES
    f32 = jnp.float32

    lane = lax.broadcasted_iota(jnp.int32, (QB, LANES), 1)
    lo_half = (lane < HEAD_DIM).astype(f32)
    hi_half = 1.0 - lo_half
    for t, ref in enumerate((qa_ref, qb_ref, qi_ref)):
        for pr in range(PAIRS):
            qp = ref[0, pr].astype(f32)
            qpad_scr[t, 2 * pr] = (qp * lo_half).astype(MXU_DTYPE)
            qpad_scr[t, 2 * pr + 1] = (qp * hi_half).astype(MXU_DTYPE)

    r_io = lax.broadcasted_iota(jnp.int32, (KC, QB), 0)
    c_io = lax.broadcasted_iota(jnp.int32, (KC, QB), 1)
    pos = (r_io & (SUBLANES - 1)) * nslab + (r_io >> 3)
    mask_lt = pos < c_io
    mask_le = pos <= c_io

    diag0 = pl.multiple_of(i * KC, KC)

    def sb_head(h, carry_unused):
        pr = h // 2
        q = qpad_scr[0, h]

        def chunk(kc, vt, mask, carry, acc, n_j, pad_rows):
            z = _dot_nt(kc, q)
            sp = jnp.maximum(z, 0.0) + jnp.log(1.0 + jnp.exp(-jnp.abs(z)))
            if mask is not None:
                sp = jnp.where(mask, sp, 0.0)
            sp3 = sp.reshape(n_j, SUBLANES, QB)
            z3 = z.reshape(n_j, SUBLANES, QB)
            run = jnp.zeros((SUBLANES, QB), f32)
            suffix = [None] * n_j
            for j in reversed(range(n_j)):
                run = run + sp3[j]
                suffix[j] = run
            off = _group_excl_scan(run, reverse=True) + carry
            a = jnp.concatenate([jnp.exp(z3[j] - suffix[j] - off) for j in range(n_j)], axis=0)
            if mask is not None:
                a = jnp.where(mask, a, 0.0)
            a = a.astype(MXU_DTYPE)
            if pad_rows:
                a = jnp.concatenate([a, jnp.zeros((pad_rows, QB), MXU_DTYPE)], axis=0)
            acc = acc + _dot(vt, a)
            carry = carry + jnp.sum(run, axis=0, keepdims=True)
            return carry, acc

        carry = jnp.zeros((1, QB), f32)
        acc = jnp.zeros((HEAD_DIM, QB), f32)
        carry, acc = chunk(ka_ref[0, pr, pl.ds(diag0, KC), :], vat_ref[0, h, i], mask_lt,
                           carry, acc, nslab, 0)

        def off_body(t, st):
            c = i - 1 - t
            r0 = pl.multiple_of(c * KC, KC)
            return chunk(ka_ref[0, pr, pl.ds(r0, KC), :], vat_ref[0, h, c], None,
                         st[0], st[1], nslab, 0)

        carry, acc = lax.fori_loop(0, i, off_body, (carry, acc))
        carry, acc = chunk(kam_ref[pr], vatm_ref[h], None, carry, acc, mslab, LANES - N_META)
        ya_ref[0, h] = acc.astype(ya_ref.dtype)
        return carry_unused

    lax.fori_loop(0, HEADS, sb_head, 0)

    w = wi_ref[0]
    lo = jnp.where(w >= 0.0, 0.0, -jnp.inf)
    hi = jnp.where(w >= 0.0, jnp.inf, 0.0)

    def score_keys(kic, mask):
        acc = jnp.zeros((kic.shape[0], QB), f32)
        for h in range(HEADS):
            s = _dot_nt(kic, qpad_scr[2, h])
            acc = acc + jnp.minimum(jnp.maximum(s, lo[h:h + 1, :]), hi[h:h + 1, :])
        if mask is not None:
            acc = jnp.where(mask, acc, -jnp.inf)
        bits = lax.bitcast_convert_type(acc, jnp.int32)
        key = jnp.where(bits < 0, bits ^ jnp.int32(0x7FFFFFFF), bits)
        return jnp.where(bits == jnp.int32(INT_MIN), 0, key)

    def score_body(c, _):
        r0 = pl.multiple_of(c * KC, KC)
        key_scr[pl.ds(r0, KC), :] = score_keys(ki_ref[0, pl.ds(r0, KC), :], None)
        return 0

    lax.fori_loop(0, i, score_body, 0)
    key_scr[pl.ds(diag0, KC), :] = score_keys(ki_ref[0, pl.ds(diag0, KC), :], mask_le)
    key_scr[seq:seq + N_META, :] = score_keys(kim_ref[...], None)

    def count(cand, strict):
        cb = jnp.broadcast_to(cand, (SUBLANES, QB))

        def add_rows(blk, n_j, accs):
            b3 = blk.reshape(n_j, SUBLANES, QB)
            accs = list(accs)
            for j in range(n_j):
                hit = (b3[j] > cb) if strict else (b3[j] >= cb)
                accs[j % 4] = accs[j % 4] + jnp.where(hit, 1.0, 0.0)
            return tuple(accs)

        def body(c, accs):
            r0 = pl.multiple_of(c * KC, KC)
            return add_rows(key_scr[pl.ds(r0, KC), :], nslab, accs)

        zero = jnp.zeros((SUBLANES, QB), f32)
        accs = lax.fori_loop(0, i + 1, body, (zero, zero, zero, zero))
        accs = add_rows(key_scr[seq:seq + N_META, :], mslab, accs)
        tot = (accs[0] + accs[1]) + (accs[2] + accs[3])
        return jnp.sum(tot, axis=0, keepdims=True)

    def bisect(it, ans):
        cand = ans ^ lax.shift_left(jnp.int32(1), 31 - it)
        return jnp.where(count(cand, False) >= float(topk), cand, ans)

    thr = lax.fori_loop(0, 32, bisect, jnp.full((1, QB), INT_MIN, jnp.int32))

    need = float(topk) - count(thr, True)
    thr_b = jnp.broadcast_to(thr, (SUBLANES, QB))
    need_b = jnp.broadcast_to(need, (SUBLANES, QB))

    def select_rows(blk, n_j, mask, carry):
        b3 = blk.reshape(n_j, SUBLANES, QB)
        tie = [jnp.where(b3[j] == thr_b, 1.0, 0.0) for j in range(n_j)]
        run = jnp.zeros((SUBLANES, QB), f32)
        prefix = []
        for j in range(n_j):
            run = run + tie[j]
            prefix.append(run)
        off = _group_excl_scan(run, reverse=False) + carry
        rows = []
        for j in range(n_j):
            rank = prefix[j] - tie[j] + off
            sel = (b3[j] > thr_b) | ((tie[j] > 0.0) & (rank < need_b))
            rows.append(jnp.where(sel, 0.0, NEG))
        mb = jnp.concatenate(rows, axis=0)
        if mask is not None:
            mb = jnp.where(mask, mb, NEG)
        return mb, carry + jnp.sum(run, axis=0, keepdims=True)

    mb, tie_carry = select_rows(key_scr[seq:seq + N_META, :], mslab, None, jnp.zeros((1, QB), f32))
    mb_scr[seq:seq + N_META, :] = mb

    def select_body(c, carry):
        r0 = pl.multiple_of(c * KC, KC)
        mb_c, carry = select_rows(key_scr[pl.ds(r0, KC), :], nslab, None, carry)
        mb_scr[pl.ds(r0, KC), :] = mb_c
        return carry

    tie_carry = lax.fori_loop(0, i, select_body, tie_carry)
    mb, _ = select_rows(key_scr[pl.ds(diag0, KC), :], nslab, mask_le, tie_carry)
    mb_scr[pl.ds(diag0, KC), :] = mb

    has_prev = i > 0
    prev0 = pl.multiple_of(jnp.maximum(i - 1, 0) * KC, KC)
    first_block = (i == 0).astype(f32)

    def dsa_head(h, carry_unused):
        q = qpad_scr[1, h]

        def chunk(kc, vt, mbias, st, pad_rows):
            m, l, acc = st
            lg = _dot_nt(kc, q) + mbias
            m_new = jnp.maximum(m, jnp.max(lg, axis=0, keepdims=True))
            alpha = jnp.exp(m - m_new)
            p = jnp.exp(lg - m_new)
            l = l * alpha + jnp.sum(p, axis=0, keepdims=True)
            pb = p.astype(MXU_DTYPE)
            if pad_rows:
                pb = jnp.concatenate([pb, jnp.zeros((pad_rows, QB), MXU_DTYPE)], axis=0)
            acc = acc * alpha + _dot(vt, pb)
            return m_new, l, acc

        st = (jnp.full((1, QB), -jnp.inf, f32), jnp.zeros((1, QB), f32),
              jnp.zeros((HEAD_DIM, QB), f32))
        st = chunk(kb_ref[0, pl.ds(diag0, KC), :], vbt_ref[0, i],
                   mb_scr[pl.ds(diag0, KC), :] + biasd_ref[h], st, 0)
        mb_prev = jnp.where(has_prev, mb_scr[pl.ds(prev0, KC), :], NEG) + biasp_ref[h]
        st = chunk(kb_ref[0, pl.ds(prev0, KC), :], vbt_ref[0, jnp.maximum(i - 1, 0)], mb_prev, st, 0)

        def far_body(c, st):
            r0 = pl.multiple_of(c * KC, KC)
            return chunk(kb_ref[0, pl.ds(r0, KC), :], vbt_ref[0, c], mb_scr[pl.ds(r0, KC), :], st, 0)

        st = lax.fori_loop(0, i - 1, far_body, st)
        st = chunk(kbm_ref[...], vbtm_ref[...],
                   mb_scr[seq:seq + N_META, :] + biasm_ref[h] * first_block, st, LANES - N_META)
        m, l, acc = st
        yb_ref[0, h] = (acc / l).astype(yb_ref.dtype)
        return carry_unused

    lax.fori_loop(0, HEADS, dsa_head, 0)


def _attn_call(qa, qb, qi, wit, ka, vat, ki, kb, vbt, kam, vatm, kim, kbm, vbtm,
               bias_d, bias_p, bias_m, topk):
    bsz, _, s, _ = qa.shape
    nq = s // QB
    nc = s // KC
    bf = MXU_DTYPE
    qspec = pl.BlockSpec((1, PAIRS, QB, LANES), lambda b, i: (b, 0, i, 0))
    const = lambda shape: pl.BlockSpec(shape, lambda b, i: (0,) * len(shape))
    yspec = pl.BlockSpec((1, HEADS, HEAD_DIM, QB), lambda b, i: (b, 0, 0, i))
    yshape = jax.ShapeDtypeStruct((bsz, HEADS, HEAD_DIM, s), bf)
    return pl.pallas_call(
        functools.partial(_attn_kernel, seq=s, topk=topk),
        grid=(bsz, nq),
        in_specs=[qspec, qspec, qspec,
                  pl.BlockSpec((1, HEADS, QB), lambda b, i: (b, 0, i)),
                  pl.BlockSpec((1, PAIRS, s, LANES), lambda b, i: (b, 0, 0, 0)),
                  pl.BlockSpec((1, HEADS, nc, HEAD_DIM, KC), lambda b, i: (b, 0, 0, 0, 0)),
                  pl.BlockSpec((1, s, LANES), lambda b, i: (b, 0, 0)),
                  pl.BlockSpec((1, s, LANES), lambda b, i: (b, 0, 0)),
                  pl.BlockSpec((1, nc, HEAD_DIM, KC), lambda b, i: (b, 0, 0, 0)),
                  const((PAIRS, N_META, LANES)), const((HEADS, HEAD_DIM, LANES)),
                  const((N_META, LANES)), const((N_META, LANES)), const((HEAD_DIM, LANES)),
                  const((HEADS, KC, QB)), const((HEADS, KC, QB)), const((HEADS, N_META, QB))],
        out_specs=[yspec, yspec],
        out_shape=[yshape, yshape],
        scratch_shapes=[pltpu.VMEM((3, HEADS, QB, LANES), bf),
                        pltpu.VMEM((s + N_META, QB), jnp.int32),
                        pltpu.VMEM((s + N_META, QB), jnp.float32)],
        compiler_params=pltpu.CompilerParams(
            dimension_semantics=("arbitrary", "arbitrary"), vmem_limit_bytes=VMEM_LIMIT),
        name="attn",
    )(qa, qb, qi, wit, ka, vat, ki, kb, vbt, kam, vatm, kim, kbm, vbtm, bias_d, bias_p, bias_m)


def _sigmoid(x):
    return 1.0 / (1.0 + jnp.exp(-x))


def _merge_kernel(x_ref, ya_ref, yb_ref, g_in_ref, b_in_ref, wz_ref, bg_ref, wpa_ref, wpb_ref,
                  wo_ref, g_ref, b_ref, o_ref):
    h = _layer_norm(x_ref[0], g_in_ref[...], b_in_ref[...])
    hb = h.astype(MXU_DTYPE)

    def branch(y_ref, z_col, g_col, bg_col, wp_ref):
        z = _dot(hb, wz_ref[:, z_col:z_col + WIDTH])
        y = (y_ref[0].astype(jnp.float32) * (z * _sigmoid(z))).astype(MXU_DTYPE)
        gate = _sigmoid(_dot(hb, wz_ref[:, g_col:g_col + D_MODEL]) + bg_ref[:, bg_col:bg_col + D_MODEL])
        return gate * _dot(y, wp_ref[...])

    merged = (branch(ya_ref, 0, 2 * WIDTH, 0, wpa_ref)
              + branch(yb_ref, WIDTH, 2 * WIDTH + D_MODEL, D_MODEL, wpb_ref))
    out = _dot(merged.astype(MXU_DTYPE), wo_ref[...])
    o_ref[0] = _layer_norm(DEEPNORM_ALPHA * h + out, g_ref[...], b_ref[...])


def _merge_call(x, ya, yb, ln_in_g, ln_in_b, wz, bg, wpa, wpb, wo, ln_g, ln_b, tm):
    bsz, s, d = x.shape
    assert s % tm == 0
    const = lambda shape: pl.BlockSpec(shape, lambda b, i: (0,) * len(shape))
    row = lambda width: pl.BlockSpec((1, tm, width), lambda b, i: (b, i, 0))
    return pl.pallas_call(
        _merge_kernel,
        grid=(bsz, s // tm),
        in_specs=[row(d), row(WIDTH), row(WIDTH), const((1, d)), const((1, d)),
                  const((d, 2 * WIDTH + 2 * d)), const((1, 2 * d)),
                  const((WIDTH, d)), const((WIDTH, d)), const((d, d)), const((1, d)), const((1, d))],
        out_specs=row(d),
        out_shape=jax.ShapeDtypeStruct((bsz, s, d), jnp.float32),
        compiler_params=pltpu.CompilerParams(
            dimension_semantics=("arbitrary", "arbitrary"), vmem_limit_bytes=VMEM_LIMIT),
        name="merge",
    )(x, ya, yb, ln_in_g, ln_in_b, wz, bg, wpa, wpb, wo, ln_g, ln_b)


def _perm_rows(a, axis, chunk):
    shp = a.shape
    n = shp[axis] // chunk
    a = a.reshape(shp[:axis] + (n, SUBLANES, chunk // SUBLANES) + shp[axis + 1:])
    a = jnp.swapaxes(a, axis + 1, axis + 2)
    return a.reshape(shp)


def _rel_bucket(dist):
    max_exact = REL_BUCKETS // 2
    nf = jnp.maximum(dist, 1).astype(jnp.float32)
    large = max_exact + (jnp.log(nf / max_exact) / math.log(REL_MAX_DIST / max_exact)
                         * (REL_BUCKETS - max_exact)).astype(jnp.int32)
    large = jnp.minimum(large, REL_BUCKETS - 1)
    return jnp.where(dist < max_exact, dist, large)


def _bias_tiles(rel_bias):
    dist = jnp.arange(2 * KC + N_META, dtype=jnp.int32)
    tab = rel_bias[_rel_bucket(dist)] - rel_bias[REL_BUCKETS - 1]
    r = jnp.arange(KC, dtype=jnp.int32)
    pos = (r % SUBLANES) * (KC // SUBLANES) + r // SUBLANES
    q = jnp.arange(QB, dtype=jnp.int32)
    d_diag = jnp.maximum(q[None, :] - pos[:, None], 0)
    d_prev = KC + q[None, :] - pos[:, None]
    rm = jnp.arange(N_META, dtype=jnp.int32)
    posm = (rm % SUBLANES) * (N_META // SUBLANES) + rm // SUBLANES
    d_meta = N_META + q[None, :] - posm[:, None]
    tile = lambda d: jnp.transpose(tab[d], (2, 0, 1)).astype(jnp.float32)
    return tile(d_diag), tile(d_prev), tile(d_meta)


def kernel(x, meta_tokens, ln_in_g, ln_in_b, rel_bias, w_in, b_gate, idx_kn_g, idx_kn_b,
           w_pa, w_pb, w_o, ln_g, ln_b):
    bsz, seq, d = x.shape
    assert d == D_MODEL and seq % KC == 0 and w_in.shape[0] == DEPTH == 1
    topk = min(TOPK_MAX, seq // 4)
    nc = seq // KC
    bf = MXU_DTYPE
    w = w_in[0]
    dup = lambda a: jnp.concatenate([a, a], axis=-1)
    w_wi = jnp.repeat(w[:, O_WI:O_WI + HEADS], HEAD_DIM, axis=1)
    w_proj = jnp.concatenate(
        [w[:, O_QA:O_QA + WIDTH], w[:, O_KA:O_KA + WIDTH], w[:, O_VA:O_VA + WIDTH],
         w[:, O_QB:O_QB + WIDTH], w[:, O_QI:O_QI + WIDTH], w_wi,
         dup(w[:, O_KI:O_KI + HEAD_DIM]), dup(w[:, O_KB:O_KB + HEAD_DIM]),
         dup(w[:, O_VB:O_VB + HEAD_DIM]),
         jnp.pad(w[:, O_WI:O_WI + HEADS], ((0, 0), (0, LANES - HEADS)))], axis=1).astype(bf)
    w_z = jnp.concatenate([w[:, O_ZA:O_ZA + WIDTH], w[:, O_ZB:O_ZB + WIDTH],
                           w[:, O_GA:O_GA + D_MODEL], w[:, O_GB:O_GB + D_MODEL]], axis=1).astype(bf)
    row = lambda a: a.reshape(1, -1).astype(jnp.float32)
    kn_g2, kn_b2 = row(dup(idx_kn_g[0])), row(dup(idx_kn_b[0]))

    proj = functools.partial(_proj_call, ln_g=row(ln_in_g), ln_b=row(ln_in_b), w_proj=w_proj,
                             kn_g2=kn_g2, kn_b2=kn_b2)
    qa, ka, va, qb, qi, ki, kb, vb, wi = proj(x, tm=min(512, KC * math.gcd(nc, 2)))
    _, kam, vam, _, _, kim, kbm, vbm, _ = proj(meta_tokens[None].astype(x.dtype), tm=N_META)

    ka = _perm_rows(ka, 2, KC)
    ki = _perm_rows(ki, 1, KC)
    kb = _perm_rows(kb, 1, KC)
    vat = _perm_rows(va, 1, KC).reshape(bsz, nc, KC, HEADS, HEAD_DIM).transpose(0, 3, 1, 4, 2)
    vbt = _perm_rows(vb[:, :, :HEAD_DIM], 1, KC).reshape(bsz, nc, KC, HEAD_DIM).transpose(0, 1, 3, 2)
    wit = jnp.swapaxes(wi[:, :, :HEADS], 1, 2)
    kam = _perm_rows(kam[0], 1, N_META)
    kim = _perm_rows(kim[0], 0, N_META)
    kbm = _perm_rows(kbm[0], 0, N_META)
    pad_keys = lambda a: jnp.pad(a, [(0, 0)] * (a.ndim - 1) + [(0, LANES - N_META)])
    vatm = pad_keys(_perm_rows(vam[0], 0, N_META).reshape(N_META, HEADS, HEAD_DIM).transpose(1, 2, 0))
    vbtm = pad_keys(_perm_rows(vbm[0, :, :HEAD_DIM], 0, N_META).T)
    bias_d, bias_p, bias_m = _bias_tiles(rel_bias)

    yat, ybt = _attn_call(qa, qb, qi, wit, ka, vat, ki, kb, vbt, kam, vatm, kim, kbm, vbtm,
                          bias_d, bias_p, bias_m, topk)
    ya = yat.reshape(bsz, WIDTH, seq).transpose(0, 2, 1)
    yb = ybt.reshape(bsz, WIDTH, seq).transpose(0, 2, 1)

    return _merge_call(x, ya, yb, row(ln_in_g), row(ln_in_b), w_z, row(b_gate[0]),
                       w_pa[0].astype(bf), w_pb[0].astype(bf), w_o[0].astype(bf),
                       row(ln_g[0]), row(ln_b[0]), tm=min(256, seq))
```

```python
import functools
import math

import jax
import jax.numpy as jnp
import numpy as np
from jax import lax
from jax.experimental import pallas as pl
from jax.experimental.pallas import tpu as pltpu

D_MODEL = 1024
DEPTH = 1
N_META = 16
HEADS = 8
HEAD_DIM = 64
WIDTH = HEADS * HEAD_DIM
TOPK_MAX = 256
REL_BUCKETS = 32
REL_MAX_DIST = 128
LN_EPS = 1e-5
DEEPNORM_ALPHA = (2.0 * DEPTH) ** 0.25
IDX_SCALE = HEADS ** -0.5 * HEAD_DIM ** -0.5
QK_SCALE = HEAD_DIM ** -0.5
LOG2E = math.log2(math.e)
QK_SCALE_LOG2 = QK_SCALE * LOG2E

SUBLANES = 8
LANES = 128
QB = 256
KC = 256
PAIRS = HEADS // 2
NEG = -1e30
INT_MIN = -2 ** 31
VMEM_LIMIT = 56 * 1024 * 1024
MXU_DTYPE = jnp.bfloat16

_COLS = [WIDTH, WIDTH, WIDTH, WIDTH, WIDTH, HEAD_DIM, HEAD_DIM, WIDTH, WIDTH, HEAD_DIM, HEADS,
         D_MODEL, D_MODEL]
_OFF = [0]
for _c in _COLS:
    _OFF.append(_OFF[-1] + _c)
(O_QA, O_KA, O_VA, O_ZA, O_QB, O_KB, O_VB, O_ZB, O_QI, O_KI, O_WI, O_GA, O_GB, O_END) = _OFF


def _layer_norm(x, g, b):
    mu = jnp.mean(x, axis=-1, keepdims=True)
    xc = x - mu
    var = jnp.mean(xc * xc, axis=-1, keepdims=True)
    return xc * lax.rsqrt(var + LN_EPS) * g + b


def _dot(a, b):
    return jnp.dot(a, b, preferred_element_type=jnp.float32)


def _dot_nt(a, b):
    return lax.dot_general(a, b, (((1,), (1,)), ((), ())), preferred_element_type=jnp.float32)


W_PROJ_COLS = 6 * WIDTH + 4 * LANES


def _proj_kernel(x_ref, g_ref, b_ref, w_ref, kng_ref, knb_ref,
                 qa_ref, ka_ref, va_ref, qb_ref, qi_ref, ki_ref, kb_ref, vb_ref, wi_ref):
    h = _layer_norm(x_ref[0], g_ref[...], b_ref[...]).astype(MXU_DTYPE)

    def cols(k, width=WIDTH):
        return _dot(h, w_ref[:, k:k + width])

    def store_pairs(ref, val):
        for pr in range(PAIRS):
            ref[0, pr] = val[:, pr * LANES:(pr + 1) * LANES].astype(ref.dtype)

    store_pairs(qa_ref, cols(0) * QK_SCALE_LOG2)
    store_pairs(ka_ref, cols(WIDTH))
    va_ref[0] = cols(2 * WIDTH).astype(va_ref.dtype)
    store_pairs(qb_ref, cols(3 * WIDTH) * QK_SCALE_LOG2)
    store_pairs(qi_ref, cols(4 * WIDTH) * cols(5 * WIDTH) * IDX_SCALE)
    base = 6 * WIDTH
    ki_ref[0] = _layer_norm(cols(base, LANES), kng_ref[...], knb_ref[...]).astype(ki_ref.dtype)
    kb_ref[0] = cols(base + LANES, LANES).astype(kb_ref.dtype)
    vb_ref[0] = cols(base + 2 * LANES, LANES).astype(vb_ref.dtype)
    wi_ref[0] = cols(base + 3 * LANES, LANES)


def _proj_call(x, ln_g, ln_b, w_proj, kn_g2, kn_b2, tm):
    bsz, s, d = x.shape
    assert s % tm == 0
    grid = (bsz, s // tm)
    bf = MXU_DTYPE
    pair_shape = jax.ShapeDtypeStruct((bsz, PAIRS, s, LANES), bf)
    pair_spec = pl.BlockSpec((1, PAIRS, tm, LANES), lambda b, i: (b, 0, i, 0))
    row128 = pl.BlockSpec((1, tm, LANES), lambda b, i: (b, i, 0))
    const = lambda shape: pl.BlockSpec(shape, lambda b, i: (0,) * len(shape))
    return pl.pallas_call(
        _proj_kernel,
        grid=grid,
        in_specs=[pl.BlockSpec((1, tm, d), lambda b, i: (b, i, 0)),
                  const((1, d)), const((1, d)), const((d, W_PROJ_COLS)),
                  const((1, LANES)), const((1, LANES))],
        out_specs=[pair_spec, pair_spec,
                   pl.BlockSpec((1, tm, WIDTH), lambda b, i: (b, i, 0)),
                   pair_spec, pair_spec, row128, row128, row128, row128],
        out_shape=[pair_shape, pair_shape, jax.ShapeDtypeStruct((bsz, s, WIDTH), bf),
                   pair_shape, pair_shape,
                   jax.ShapeDtypeStruct((bsz, s, LANES), bf),
                   jax.ShapeDtypeStruct((bsz, s, LANES), bf),
                   jax.ShapeDtypeStruct((bsz, s, LANES), bf),
                   jax.ShapeDtypeStruct((bsz, s, LANES), jnp.float32)],
        compiler_params=pltpu.CompilerParams(
            dimension_semantics=("arbitrary", "arbitrary"), vmem_limit_bytes=VMEM_LIMIT),
        name="proj",
    )(x, ln_g, ln_b, w_proj, kn_g2, kn_b2)


def _group_excl_scan(g_tot, reverse):
    sub = lax.broadcasted_iota(jnp.int32, g_tot.shape, 0)
    out = jnp.zeros_like(g_tot)
    for g in range(SUBLANES):
        row = g_tot[g:g + 1, :]
        take = (sub < g) if reverse else (sub > g)
        out = out + jnp.where(take, row, 0.0)
    return out


def _bit_planes(key):
    assert key.shape[0] == 32 * SUBLANES
    k3 = key.reshape(32, SUBLANES, key.shape[1])
    a = [k3[j] ^ jnp.int32(INT_MIN) for j in range(32)]
    as_i32 = lambda m: jnp.int32(m - (1 << 32) if m >= (1 << 31) else m)
    j, m = 16, 0x0000FFFF
    while j:
        shift = jnp.full(a[0].shape, j, jnp.int32)
        k = 0
        while k < 32:
            t = (a[k] ^ lax.shift_right_logical(a[k + j], shift)) & as_i32(m)
            a[k] = a[k] ^ t
            a[k + j] = a[k + j] ^ lax.shift_left(t, shift)
            k = (k + j + 1) & ~j
        j >>= 1
        m = (m ^ (m << j)) & 0xFFFFFFFF
    return a


def _attn_kernel(qa_ref, qb_ref, qi_ref, wi_ref,
                 ka_ref, vat_ref, ki_ref, kb_ref, vbt_ref,
                 kam_ref, vatm_ref, kim_ref, kbm_ref, vbtm_ref,
                 biasd_ref, biasp_ref, biasm_ref,
                 ya_ref, yb_ref,
                 qpad_scr, key_scr, mb_scr, sbacc_scr, sbcar_scr, dacc_scr, m_scr, l_scr, z_scr,
                 plane_scr, alive_scr, ones_scr, *, seq, topk):
    i = pl.program_id(1)
    nslab = KC // SUBLANES
    mslab = N_META // SUBLANES
    f32 = jnp.float32

    lane = lax.broadcasted_iota(jnp.int32, (QB, LANES), 1)
    lo_half = (lane < HEAD_DIM).astype(f32)
    hi_half = 1.0 - lo_half
    for t, ref in enumerate((qa_ref, qb_ref, qi_ref)):
        for pr in range(PAIRS):
            qp = ref[0, pr].astype(f32)
            qpad_scr[t, 2 * pr] = (qp * lo_half).astype(MXU_DTYPE)
            qpad_scr[t, 2 * pr + 1] = (qp * hi_half).astype(MXU_DTYPE)

    r_io = lax.broadcasted_iota(jnp.int32, (KC, QB), 0)
    c_io = lax.broadcasted_iota(jnp.int32, (KC, QB), 1)
    pos = (r_io & (SUBLANES - 1)) * nslab + (r_io >> 3)
    mask_lt = pos < c_io
    mask_le = pos <= c_io

    diag0 = pl.multiple_of(i * KC, KC)

    def heads_pipelined(logits_of_head, rest_of_head, rows):
        for h in range(HEADS):
            z_scr[h, 0:rows] = logits_of_head(h)
        for h in range(HEADS):
            rest_of_head(h, z_scr[h, 0:rows])()

    def sb_chunk(k_of_pair, vt_of_head, mask, n_j, pad_rows, first, last):
        def logits(h):
            return _dot_nt(k_of_pair(h // 2), qpad_scr[0, h])

        def rest(h, z):
            neg_abs = lax.bitcast_convert_type(
                lax.bitcast_convert_type(z, jnp.int32) | jnp.int32(INT_MIN), f32)
            sp = jnp.maximum(z, 0.0) + jnp.log2(1.0 + jnp.exp2(neg_abs))
            if mask is not None:
                sp = jnp.where(mask, sp, 0.0)
            sp3 = sp.reshape(n_j, SUBLANES, QB)
            z3 = z.reshape(n_j, SUBLANES, QB)
            run = jnp.zeros((SUBLANES, QB), f32)
            u = [None] * n_j
            for j in reversed(range(n_j)):
                run = run + sp3[j]
                u[j] = jnp.exp2(z3[j] - run)
            carry = jnp.zeros((SUBLANES, QB), f32) if first else sbcar_scr[h]
            scale = jnp.exp2(-(_group_excl_scan(run, reverse=True) + carry))
            a = jnp.concatenate([u[j] * scale for j in range(n_j)], axis=0)
            if mask is not None:
                a = jnp.where(mask, a, 0.0)
            a = a.astype(MXU_DTYPE)
            if pad_rows:
                a = jnp.concatenate([a, jnp.zeros((pad_rows, QB), MXU_DTYPE)], axis=0)
            if not last:
                sbcar_scr[h] = carry + jnp.sum(run, axis=0, keepdims=True)

            def finish():
                pv = _dot(vt_of_head(h), a)
                acc = pv if first else sbacc_scr[h] + pv
                if last:
                    ya_ref[0, h] = acc.astype(ya_ref.dtype)
                else:
                    sbacc_scr[h] = acc

            return finish

        heads_pipelined(logits, rest, n_j * SUBLANES)

    sb_chunk(lambda pr: ka_ref[0, pr, pl.ds(diag0, KC), :], lambda h: vat_ref[0, h, i],
             mask_lt, nslab, 0, True, False)

    def sb_off(t, _):
        c = i - 1 - t
        r0 = pl.multiple_of(c * KC, KC)
        sb_chunk(lambda pr: ka_ref[0, pr, pl.ds(r0, KC), :], lambda h: vat_ref[0, h, c],
                 None, nslab, 0, False, False)
        return 0

    lax.fori_loop(0, i, sb_off, 0)
    sb_chunk(lambda pr: kam_ref[pr], lambda h: vatm_ref[h], None, mslab, LANES - N_META, False, True)

    w = wi_ref[0]
    lo = jnp.where(w >= 0.0, 0.0, -jnp.inf)
    hi = jnp.where(w >= 0.0, jnp.inf, 0.0)

    def score_keys(kic, mask):
        acc = jnp.zeros((kic.shape[0], QB), f32)
        for h in range(HEADS):
            s = _dot_nt(kic, qpad_scr[2, h])
            acc = acc + jnp.minimum(jnp.maximum(s, lo[h:h + 1, :]), hi[h:h + 1, :])
        if mask is not None:
            acc = jnp.where(mask, acc, -jnp.inf)
        bits = lax.bitcast_convert_type(acc, jnp.int32)
        key = jnp.where(bits < 0, bits ^ jnp.int32(0x7FFFFFFF), bits)
        return jnp.where(bits == jnp.int32(INT_MIN), 0, key)

    ones_word = jnp.full((SUBLANES, QB), -1, jnp.int32)

    def store_chunk_keys(c, r0, mask):
        key = score_keys(ki_ref[0, pl.ds(r0, KC), :], mask)
        key_scr[pl.ds(r0, KC), :] = key
        planes = _bit_planes(key)
        for o in range(32):
            plane_scr[c, o] = planes[o]
        alive_scr[c] = ones_word
        ones_scr[c] = ones_word

    def score_body(c, _):
        store_chunk_keys(c, pl.multiple_of(c * KC, KC), None)
        return 0

    lax.fori_loop(0, i, score_body, 0)
    store_chunk_keys(i, diag0, mask_le)
    meta_keys = score_keys(kim_ref[...], None)
    key_scr[seq:seq + N_META, :] = meta_keys

    def count_meta(cand, strict):
        cb = jnp.broadcast_to(cand, (N_META, QB))
        hit = (meta_keys > cb) if strict else (meta_keys >= cb)
        return jnp.sum(jnp.where(hit, 1.0, 0.0), axis=0, keepdims=True)

    def radix_step(it, st):
        ans, above, took = st
        cand = ans ^ lax.shift_left(jnp.int32(1), 31 - it)
        took_b = jnp.broadcast_to(took, (SUBLANES, QB)) != 0

        def body(c, acc):
            ones = ones_scr[c]
            alive = jnp.where(took_b, ones, alive_scr[c] ^ ones)
            ones = alive & plane_scr[c, it]
            alive_scr[c] = alive
            ones_scr[c] = ones
            return acc + lax.population_count(ones)

        acc = lax.fori_loop(0, i + 1, body, jnp.zeros((SUBLANES, QB), jnp.int32))
        ones_cnt = jnp.sum(acc.astype(f32), axis=0, keepdims=True)
        take = (above + ones_cnt + count_meta(cand, False)) >= float(topk)
        return (jnp.where(take, cand, ans), jnp.where(take, above, above + ones_cnt),
                jnp.where(take, -1, 0).astype(jnp.int32))

    thr, above, _ = lax.fori_loop(
        0, 32, radix_step,
        (jnp.full((1, QB), INT_MIN, jnp.int32), jnp.zeros((1, QB), f32), jnp.full((1, QB), -1, jnp.int32)))

    need = float(topk) - (above + count_meta(thr, True))
    thr_b = jnp.broadcast_to(thr, (SUBLANES, QB))
    need_b = jnp.broadcast_to(need, (SUBLANES, QB))

    def select_rows(blk, n_j, mask, carry):
        b3 = blk.reshape(n_j, SUBLANES, QB)
        tie = [jnp.where(b3[j] == thr_b, 1.0, 0.0) for j in range(n_j)]
        run = jnp.zeros((SUBLANES, QB), f32)
        prefix = []
        for j in range(n_j):
            run = run + tie[j]
            prefix.append(run)
        off = _group_excl_scan(run, reverse=False) + carry
        rows = []
        for j in range(n_j):
            rank = prefix[j] - tie[j] + off
            sel = (b3[j] > thr_b) | ((tie[j] > 0.0) & (rank < need_b))
            rows.append(jnp.where(sel, 0.0, NEG))
        mb = jnp.concatenate(rows, axis=0)
        if mask is not None:
            mb = jnp.where(mask, mb, NEG)
        return mb, carry + jnp.sum(run, axis=0, keepdims=True)

    mb, tie_carry = select_rows(key_scr[seq:seq + N_META, :], mslab, None, jnp.zeros((1, QB), f32))
    mb_scr[seq:seq + N_META, :] = mb

    def select_body(c, carry):
        r0 = pl.multiple_of(c * KC, KC)
        mb_c, carry = select_rows(key_scr[pl.ds(r0, KC), :], nslab, None, carry)
        mb_scr[pl.ds(r0, KC), :] = mb_c
        return carry

    tie_carry = lax.fori_loop(0, i, select_body, tie_carry)
    mb, _ = select_rows(key_scr[pl.ds(diag0, KC), :], nslab, mask_le, tie_carry)
    mb_scr[pl.ds(diag0, KC), :] = mb

    def dsa_chunk(kc, vt, mbias, bias_of_head, pad_rows, first, last):
        def logits(h):
            return _dot_nt(kc, qpad_scr[1, h])

        def rest(h, qk):
            lg = qk + mbias
            if bias_of_head is not None:
                lg = lg + bias_of_head(h)
            cmax = jnp.max(lg, axis=0, keepdims=True)
            if first:
                m_new = jnp.broadcast_to(cmax, (SUBLANES, QB))
            else:
                m_old = m_scr[h]
                m_new = jnp.maximum(m_old, cmax)
                alpha = jnp.exp2(m_old - m_new)
            p3 = lg.reshape(lg.shape[0] // SUBLANES, SUBLANES, QB)
            p = jnp.exp2(p3 - m_new[None]).reshape(lg.shape)
            psum = jnp.sum(p, axis=0, keepdims=True)
            pb = p.astype(MXU_DTYPE)
            if pad_rows:
                pb = jnp.concatenate([pb, jnp.zeros((pad_rows, QB), MXU_DTYPE)], axis=0)
            l_new = jnp.broadcast_to(psum, (SUBLANES, QB)) if first else l_scr[h] * alpha + psum
            if not last:
                m_scr[h] = m_new
                l_scr[h] = l_new

            def finish():
                pv = _dot(vt, pb)
                acc = pv if first else dacc_scr[h] * alpha[0:1, :] + pv
                if last:
                    yb_ref[0, h] = (acc / l_new[0:1, :]).astype(yb_ref.dtype)
                else:
                    dacc_scr[h] = acc

            return finish

        heads_pipelined(logits, rest, kc.shape[0])

    dsa_chunk(kb_ref[0, pl.ds(diag0, KC), :], vbt_ref[0, i], mb_scr[pl.ds(diag0, KC), :],
              lambda h: biasd_ref[h], 0, True, False)
    prev_c = jnp.maximum(i - 1, 0)
    prev0 = pl.multiple_of(prev_c * KC, KC)
    dsa_chunk(kb_ref[0, pl.ds(prev0, KC), :], vbt_ref[0, prev_c],
              jnp.where(i > 0, mb_scr[pl.ds(prev0, KC), :], NEG), lambda h: biasp_ref[h], 0, False, False)

    def dsa_far(c, _):
        r0 = pl.multiple_of(c * KC, KC)
        dsa_chunk(kb_ref[0, pl.ds(r0, KC), :], vbt_ref[0, c], mb_scr[pl.ds(r0, KC), :],
                  None, 0, False, False)
        return 0

    lax.fori_loop(0, i - 1, dsa_far, 0)
    first_block = (i == 0).astype(f32)
    dsa_chunk(kbm_ref[...], vbtm_ref[...], mb_scr[seq:seq + N_META, :],
              lambda h: biasm_ref[h] * first_block, LANES - N_META, False, True)


def _attn_call(qa, qb, qi, wit, ka, vat, ki, kb, vbt, kam, vatm, kim, kbm, vbtm,
               bias_d, bias_p, bias_m, topk):
    bsz, _, s, _ = qa.shape
    nq = s // QB
    nc = s // KC
    bf = MXU_DTYPE
    qspec = pl.BlockSpec((1, PAIRS, QB, LANES), lambda b, i: (b, 0, i, 0))
    const = lambda shape: pl.BlockSpec(shape, lambda b, i: (0,) * len(shape))
    yspec = pl.BlockSpec((1, HEADS, HEAD_DIM, QB), lambda b, i: (b, 0, 0, i))
    yshape = jax.ShapeDtypeStruct((bsz, HEADS, HEAD_DIM, s), bf)
    return pl.pallas_call(
        functools.partial(_attn_kernel, seq=s, topk=topk),
        grid=(bsz, nq),
        in_specs=[qspec, qspec, qspec,
                  pl.BlockSpec((1, HEADS, QB), lambda b, i: (b, 0, i)),
                  pl.BlockSpec((1, PAIRS, s, LANES), lambda b, i: (b, 0, 0, 0)),
                  pl.BlockSpec((1, HEADS, nc, HEAD_DIM, KC), lambda b, i: (b, 0, 0, 0, 0)),
                  pl.BlockSpec((1, s, LANES), lambda b, i: (b, 0, 0)),
                  pl.BlockSpec((1, s, LANES), lambda b, i: (b, 0, 0)),
                  pl.BlockSpec((1, nc, HEAD_DIM, KC), lambda b, i: (b, 0, 0, 0)),
                  const((PAIRS, N_META, LANES)), const((HEADS, HEAD_DIM, LANES)),
                  const((N_META, LANES)), const((N_META, LANES)), const((HEAD_DIM, LANES)),
                  const((HEADS, KC, QB)), const((HEADS, KC, QB)), const((HEADS, N_META, QB))],
        out_specs=[yspec, yspec],
        out_shape=[yshape, yshape],
        scratch_shapes=[pltpu.VMEM((3, HEADS, QB, LANES), bf),
                        pltpu.VMEM((s + N_META, QB), jnp.int32),
                        pltpu.VMEM((s + N_META, QB), jnp.float32),
                        pltpu.VMEM((HEADS, HEAD_DIM, QB), jnp.float32),
                        pltpu.VMEM((HEADS, SUBLANES, QB), jnp.float32),
                        pltpu.VMEM((HEADS, HEAD_DIM, QB), jnp.float32),
                        pltpu.VMEM((HEADS, SUBLANES, QB), jnp.float32),
                        pltpu.VMEM((HEADS, SUBLANES, QB), jnp.float32),
                        pltpu.VMEM((HEADS, KC, QB), jnp.float32),
                        pltpu.VMEM((nc, 32, SUBLANES, QB), jnp.int32),
                        pltpu.VMEM((nc, SUBLANES, QB), jnp.int32),
                        pltpu.VMEM((nc, SUBLANES, QB), jnp.int32)],
        compiler_params=pltpu.CompilerParams(
            dimension_semantics=("arbitrary", "arbitrary"), vmem_limit_bytes=VMEM_LIMIT),
        name="attn",
    )(qa, qb, qi, wit, ka, vat, ki, kb, vbt, kam, vatm, kim, kbm, vbtm, bias_d, bias_p, bias_m)


def _sigmoid(x):
    return 1.0 / (1.0 + jnp.exp(-x))


def _merge_kernel(x_ref, ya_ref, yb_ref, g_in_ref, b_in_ref, wz_ref, bg_ref, wpa_ref, wpb_ref,
                  wo_ref, g_ref, b_ref, o_ref):
    h = _layer_norm(x_ref[0], g_in_ref[...], b_in_ref[...])
    hb = h.astype(MXU_DTYPE)

    def branch(y_ref, z_col, g_col, bg_col, wp_ref):
        z = _dot(hb, wz_ref[:, z_col:z_col + WIDTH])
        y = (y_ref[0].astype(jnp.float32) * (z * _sigmoid(z))).astype(MXU_DTYPE)
        gate = _sigmoid(_dot(hb, wz_ref[:, g_col:g_col + D_MODEL]) + bg_ref[:, bg_col:bg_col + D_MODEL])
        return gate * _dot(y, wp_ref[...])

    merged = (branch(ya_ref, 0, 2 * WIDTH, 0, wpa_ref)
              + branch(yb_ref, WIDTH, 2 * WIDTH + D_MODEL, D_MODEL, wpb_ref))
    out = _dot(merged.astype(MXU_DTYPE), wo_ref[...])
    o_ref[0] = _layer_norm(DEEPNORM_ALPHA * h + out, g_ref[...], b_ref[...])


def _merge_call(x, ya, yb, ln_in_g, ln_in_b, wz, bg, wpa, wpb, wo, ln_g, ln_b, tm):
    bsz, s, d = x.shape
    assert s % tm == 0
    const = lambda shape: pl.BlockSpec(shape, lambda b, i: (0,) * len(shape))
    row = lambda width: pl.BlockSpec((1, tm, width), lambda b, i: (b, i, 0))
    return pl.pallas_call(
        _merge_kernel,
        grid=(bsz, s // tm),
        in_specs=[row(d), row(WIDTH), row(WIDTH), const((1, d)), const((1, d)),
                  const((d, 2 * WIDTH + 2 * d)), const((1, 2 * d)),
                  const((WIDTH, d)), const((WIDTH, d)), const((d, d)), const((1, d)), const((1, d))],
        out_specs=row(d),
        out_shape=jax.ShapeDtypeStruct((bsz, s, d), jnp.float32),
        compiler_params=pltpu.CompilerParams(
            dimension_semantics=("arbitrary", "arbitrary"), vmem_limit_bytes=VMEM_LIMIT),
        name="merge",
    )(x, ya, yb, ln_in_g, ln_in_b, wz, bg, wpa, wpb, wo, ln_g, ln_b)


def _perm_rows(a, axis, chunk):
    shp = a.shape
    n = shp[axis] // chunk
    a = a.reshape(shp[:axis] + (n, SUBLANES, chunk // SUBLANES) + shp[axis + 1:])
    a = jnp.swapaxes(a, axis + 1, axis + 2)
    return a.reshape(shp)


def _rel_bucket(dist):
    max_exact = REL_BUCKETS // 2
    nf = np.maximum(dist, 1).astype(np.float32)
    large = max_exact + (np.log(nf / np.float32(max_exact)) / np.float32(math.log(REL_MAX_DIST / max_exact))
                         * np.float32(REL_BUCKETS - max_exact)).astype(np.int32)
    large = np.minimum(large, REL_BUCKETS - 1)
    return np.where(dist < max_exact, dist, large)


def _bias_tiles(rel_bias):
    r = np.arange(KC)
    pos = (r % SUBLANES) * (KC // SUBLANES) + r // SUBLANES
    q = np.arange(QB)
    rm = np.arange(N_META)
    posm = (rm % SUBLANES) * (N_META // SUBLANES) + rm // SUBLANES
    table = (rel_bias - rel_bias[REL_BUCKETS - 1]).astype(jnp.float32) * LOG2E

    def tile(dist):
        onehot = jnp.asarray(_rel_bucket(dist)[..., None] == np.arange(REL_BUCKETS), jnp.float32)
        return jnp.einsum("rqb,bh->hrq", onehot, table, precision=lax.Precision.HIGHEST)

    return (tile(np.maximum(q[None, :] - pos[:, None], 0)),
            tile(KC + q[None, :] - pos[:, None]),
            tile(N_META + q[None, :] - posm[:, None]))


def kernel(x, meta_tokens, ln_in_g, ln_in_b, rel_bias, w_in, b_gate, idx_kn_g, idx_kn_b,
           w_pa, w_pb, w_o, ln_g, ln_b):
    bsz, seq, d = x.shape
    assert d == D_MODEL and seq % KC == 0 and w_in.shape[0] == DEPTH == 1
    topk = min(TOPK_MAX, seq // 4)
    nc = seq // KC
    bf = MXU_DTYPE
    w = w_in[0]
    dup = lambda a: jnp.concatenate([a, a], axis=-1)
    w_wi = jnp.repeat(w[:, O_WI:O_WI + HEADS], HEAD_DIM, axis=1)
    w_proj = jnp.concatenate(
        [w[:, O_QA:O_QA + WIDTH], w[:, O_KA:O_KA + WIDTH], w[:, O_VA:O_VA + WIDTH],
         w[:, O_QB:O_QB + WIDTH], w[:, O_QI:O_QI + WIDTH], w_wi,
         dup(w[:, O_KI:O_KI + HEAD_DIM]), dup(w[:, O_KB:O_KB + HEAD_DIM]),
         dup(w[:, O_VB:O_VB + HEAD_DIM]),
         jnp.pad(w[:, O_WI:O_WI + HEADS], ((0, 0), (0, LANES - HEADS)))], axis=1).astype(bf)
    w_z = jnp.concatenate([w[:, O_ZA:O_ZA + WIDTH], w[:, O_ZB:O_ZB + WIDTH],
                           w[:, O_GA:O_GA + D_MODEL], w[:, O_GB:O_GB + D_MODEL]], axis=1).astype(bf)
    row = lambda a: a.reshape(1, -1).astype(jnp.float32)
    kn_g2, kn_b2 = row(dup(idx_kn_g[0])), row(dup(idx_kn_b[0]))

    proj = functools.partial(_proj_call, ln_g=row(ln_in_g), ln_b=row(ln_in_b), w_proj=w_proj,
                             kn_g2=kn_g2, kn_b2=kn_b2)
    qa, ka, va, qb, qi, ki, kb, vb, wi = proj(x, tm=min(512, KC * math.gcd(nc, 2)))
    _, kam, vam, _, _, kim, kbm, vbm, _ = proj(meta_tokens[None].astype(x.dtype), tm=N_META)

    ka = _perm_rows(ka, 2, KC)
    ki = _perm_rows(ki, 1, KC)
    kb = _perm_rows(kb, 1, KC)
    vat = _perm_rows(va, 1, KC).reshape(bsz, nc, KC, HEADS, HEAD_DIM).transpose(0, 3, 1, 4, 2)
    vbt = _perm_rows(vb[:, :, :HEAD_DIM], 1, KC).reshape(bsz, nc, KC, HEAD_DIM).transpose(0, 1, 3, 2)
    wit = jnp.swapaxes(wi[:, :, :HEADS], 1, 2)
    kam = _perm_rows(kam[0], 1, N_META)
    kim = _perm_rows(kim[0], 0, N_META)
    kbm = _perm_rows(kbm[0], 0, N_META)
    pad_keys = lambda a: jnp.pad(a, [(0, 0)] * (a.ndim - 1) + [(0, LANES - N_META)])
    vatm = pad_keys(_perm_rows(vam[0], 0, N_META).reshape(N_META, HEADS, HEAD_DIM).transpose(1, 2, 0))
    vbtm = pad_keys(_perm_rows(vbm[0, :, :HEAD_DIM], 0, N_META).T)
    bias_d, bias_p, bias_m = _bias_tiles(rel_bias)

    yat, ybt = _attn_call(qa, qb, qi, wit, ka, vat, ki, kb, vbt, kam, vatm, kim, kbm, vbtm,
                          bias_d, bias_p, bias_m, topk)
    ya = yat.reshape(bsz, WIDTH, seq).transpose(0, 2, 1)
    yb = ybt.reshape(bsz, WIDTH, seq).transpose(0, 2, 1)

    return _merge_call(x, ya, yb, row(ln_in_g), row(ln_in_b), w_z, row(b_gate[0]),
                       w_pa[0].astype(bf), w_pb[0].astype(bf), w_o[0].astype(bf),
                       row(ln_g[0]), row(ln_b[0]), tm=min(256, seq))
```

```python
import functools
import math

import jax
import jax.numpy as jnp
import numpy as np
from jax import lax
from jax.experimental import pallas as pl
from jax.experimental.pallas import tpu as pltpu

D_MODEL = 1024
DEPTH = 1
N_META = 16
HEADS = 8
HEAD_DIM = 64
WIDTH = HEADS * HEAD_DIM
TOPK_MAX = 256
REL_BUCKETS = 32
REL_MAX_DIST = 128
LN_EPS = 1e-5
DEEPNORM_ALPHA = (2.0 * DEPTH) ** 0.25
IDX_SCALE = HEADS ** -0.5 * HEAD_DIM ** -0.5
QK_SCALE = HEAD_DIM ** -0.5
LOG2E = math.log2(math.e)
QK_SCALE_LOG2 = QK_SCALE * LOG2E

SUBLANES = 8
LANES = 128
QB = 256
KC = 256
PAIRS = HEADS // 2
NEG = -1e30
INT_MIN = -2 ** 31
VMEM_LIMIT = 56 * 1024 * 1024
MXU_DTYPE = jnp.bfloat16

_COLS = [WIDTH, WIDTH, WIDTH, WIDTH, WIDTH, HEAD_DIM, HEAD_DIM, WIDTH, WIDTH, HEAD_DIM, HEADS,
         D_MODEL, D_MODEL]
_OFF = [0]
for _c in _COLS:
    _OFF.append(_OFF[-1] + _c)
(O_QA, O_KA, O_VA, O_ZA, O_QB, O_KB, O_VB, O_ZB, O_QI, O_KI, O_WI, O_GA, O_GB, O_END) = _OFF


def _layer_norm(x, g, b):
    mu = jnp.mean(x, axis=-1, keepdims=True)
    xc = x - mu
    var = jnp.mean(xc * xc, axis=-1, keepdims=True)
    return xc * lax.rsqrt(var + LN_EPS) * g + b


def _dot(a, b):
    return jnp.dot(a, b, preferred_element_type=jnp.float32)


def _dot_nt(a, b):
    return lax.dot_general(a, b, (((1,), (1,)), ((), ())), preferred_element_type=jnp.float32)


WQ_COLS = 4 * WIDTH
WK_COLS = WIDTH + 2 * LANES
WV_ROWS = WIDTH + HEAD_DIM
WI_ROWS = 2 * SUBLANES


def _proj_kernel(x_ref, g_ref, b_ref, perm_ref, wq_ref, wk_ref, wvt_ref, wit_ref, kng_ref, knb_ref,
                 qa_ref, qb_ref, qi_ref, wi_ref, ka_ref, ki_ref, kb_ref, vat_ref, vbt_ref):
    h = _layer_norm(x_ref[0], g_ref[...], b_ref[...]).astype(MXU_DTYPE)
    hp = _dot(perm_ref[...], h).astype(MXU_DTYPE)

    def store_pairs(ref, val):
        for pr in range(PAIRS):
            ref[0, pr] = val[:, pr * LANES:(pr + 1) * LANES].astype(ref.dtype)

    qcols = lambda k: _dot(h, wq_ref[:, k * WIDTH:(k + 1) * WIDTH])
    store_pairs(qa_ref, qcols(0) * QK_SCALE_LOG2)
    store_pairs(qb_ref, qcols(1) * QK_SCALE_LOG2)
    store_pairs(qi_ref, qcols(2) * qcols(3) * IDX_SCALE)
    wi_ref[0] = _dot_nt(wit_ref[...], h)[0:HEADS, :]

    store_pairs(ka_ref, _dot(hp, wk_ref[:, 0:WIDTH]))
    ki_ref[0] = _layer_norm(_dot(hp, wk_ref[:, WIDTH:WIDTH + LANES]),
                            kng_ref[...], knb_ref[...]).astype(ki_ref.dtype)
    kb_ref[0] = _dot(hp, wk_ref[:, WIDTH + LANES:WK_COLS]).astype(kb_ref.dtype)
    vat = _dot_nt(wvt_ref[0:WIDTH, :], hp)
    vat_ref[0, :, 0] = vat.reshape(HEADS, HEAD_DIM, KC).astype(vat_ref.dtype)
    vbt_ref[0, 0] = _dot_nt(wvt_ref[WIDTH:WV_ROWS, :], hp).astype(vbt_ref.dtype)


def _proj_call(x, ln_g, ln_b, perm, wq, wk, wvt, wit, kn_g2, kn_b2):
    bsz, s, d = x.shape
    assert s % KC == 0
    nc = s // KC
    bf = MXU_DTYPE
    pair_shape = jax.ShapeDtypeStruct((bsz, PAIRS, s, LANES), bf)
    pair_spec = pl.BlockSpec((1, PAIRS, KC, LANES), lambda b, i: (b, 0, i, 0))
    row128 = pl.BlockSpec((1, KC, LANES), lambda b, i: (b, i, 0))
    row_shape = jax.ShapeDtypeStruct((bsz, s, LANES), bf)
    const = lambda shape: pl.BlockSpec(shape, lambda b, i: (0,) * len(shape))
    return pl.pallas_call(
        _proj_kernel,
        grid=(bsz, nc),
        in_specs=[pl.BlockSpec((1, KC, d), lambda b, i: (b, i, 0)),
                  const((1, d)), const((1, d)), const((KC, KC)),
                  const((d, WQ_COLS)), const((d, WK_COLS)), const((WV_ROWS, d)), const((WI_ROWS, d)),
                  const((1, LANES)), const((1, LANES))],
        out_specs=[pair_spec, pair_spec, pair_spec,
                   pl.BlockSpec((1, HEADS, KC), lambda b, i: (b, 0, i)),
                   pair_spec, row128, row128,
                   pl.BlockSpec((1, HEADS, 1, HEAD_DIM, KC), lambda b, i: (b, 0, i, 0, 0)),
                   pl.BlockSpec((1, 1, HEAD_DIM, KC), lambda b, i: (b, i, 0, 0))],
        out_shape=[pair_shape, pair_shape, pair_shape,
                   jax.ShapeDtypeStruct((bsz, HEADS, s), jnp.float32),
                   pair_shape, row_shape, row_shape,
                   jax.ShapeDtypeStruct((bsz, HEADS, nc, HEAD_DIM, KC), bf),
                   jax.ShapeDtypeStruct((bsz, nc, HEAD_DIM, KC), bf)],
        compiler_params=pltpu.CompilerParams(
            dimension_semantics=("arbitrary", "arbitrary"), vmem_limit_bytes=VMEM_LIMIT),
        name="proj",
    )(x, ln_g, ln_b, perm, wq, wk, wvt, wit, kn_g2, kn_b2)


def _group_excl_scan(g_tot, reverse):
    sub = lax.broadcasted_iota(jnp.int32, g_tot.shape, 0)
    out = jnp.zeros_like(g_tot)
    for g in range(SUBLANES):
        row = g_tot[g:g + 1, :]
        take = (sub < g) if reverse else (sub > g)
        out = out + jnp.where(take, row, 0.0)
    return out


def _bit_planes(key):
    assert key.shape[0] == 32 * SUBLANES
    k3 = key.reshape(32, SUBLANES, key.shape[1])
    a = [k3[j] ^ jnp.int32(INT_MIN) for j in range(32)]
    as_i32 = lambda m: jnp.int32(m - (1 << 32) if m >= (1 << 31) else m)
    j, m = 16, 0x0000FFFF
    while j:
        shift = jnp.full(a[0].shape, j, jnp.int32)
        k = 0
        while k < 32:
            t = (a[k] ^ lax.shift_right_logical(a[k + j], shift)) & as_i32(m)
            a[k] = a[k] ^ t
            a[k + j] = a[k + j] ^ lax.shift_left(t, shift)
            k = (k + j + 1) & ~j
        j >>= 1
        m = (m ^ (m << j)) & 0xFFFFFFFF
    return a


def _attn_kernel(qa_ref, qb_ref, qi_ref, wi_ref,
                 ka_ref, vat_ref, ki_ref, kb_ref, vbt_ref,
                 kam_ref, vatm_ref, kim_ref, kbm_ref, vbtm_ref,
                 biasd_ref, biasp_ref, biasm_ref,
                 ya_ref, yb_ref,
                 qpad_scr, key_scr, mb_scr, sbacc_scr, sbcar_scr, dacc_scr, m_scr, l_scr, z_scr,
                 plane_scr, alive_scr, ones_scr, *, seq, topk):
    i = pl.program_id(1)
    nslab = KC // SUBLANES
    mslab = N_META // SUBLANES
    f32 = jnp.float32

    lane = lax.broadcasted_iota(jnp.int32, (QB, LANES), 1)
    lo_half = (lane < HEAD_DIM).astype(f32)
    hi_half = 1.0 - lo_half
    for t, ref in enumerate((qa_ref, qb_ref, qi_ref)):
        for pr in range(PAIRS):
            qp = ref[0, pr].astype(f32)
            qpad_scr[t, 2 * pr] = (qp * lo_half).astype(MXU_DTYPE)
            qpad_scr[t, 2 * pr + 1] = (qp * hi_half).astype(MXU_DTYPE)

    r_io = lax.broadcasted_iota(jnp.int32, (KC, QB), 0)
    c_io = lax.broadcasted_iota(jnp.int32, (KC, QB), 1)
    pos = (r_io & (SUBLANES - 1)) * nslab + (r_io >> 3)
    mask_lt = pos < c_io
    mask_le = pos <= c_io

    diag0 = pl.multiple_of(i * KC, KC)

    def run_streams(*streams):
        for logits_of_head, _, rows, buf in streams:
            for h in range(HEADS):
                z_scr[buf, h, 0:rows] = logits_of_head(h)
        for h in range(HEADS):
            for _, rest_of_head, rows, buf in streams:
                rest_of_head(h, z_scr[buf, h, 0:rows])()

    def sb_chunk(k_of_pair, vt_of_head, mask, n_j, pad_rows, first, last):
        def logits(h):
            return _dot_nt(k_of_pair(h // 2), qpad_scr[0, h])

        def rest(h, z):
            neg_abs = lax.bitcast_convert_type(
                lax.bitcast_convert_type(z, jnp.int32) | jnp.int32(INT_MIN), f32)
            sp = jnp.maximum(z, 0.0) + jnp.log2(1.0 + jnp.exp2(neg_abs))
            if mask is not None:
                sp = jnp.where(mask, sp, 0.0)
            sp3 = sp.reshape(n_j, SUBLANES, QB)
            z3 = z.reshape(n_j, SUBLANES, QB)
            run = jnp.zeros((SUBLANES, QB), f32)
            u = [None] * n_j
            for j in reversed(range(n_j)):
                run = run + sp3[j]
                u[j] = jnp.exp2(z3[j] - run)
            carry = jnp.zeros((SUBLANES, QB), f32) if first else sbcar_scr[h]
            scale = jnp.exp2(-(_group_excl_scan(run, reverse=True) + carry))
            a = jnp.concatenate([u[j] * scale for j in range(n_j)], axis=0)
            if mask is not None:
                a = jnp.where(mask, a, 0.0)
            a = a.astype(MXU_DTYPE)
            if pad_rows:
                a = jnp.concatenate([a, jnp.zeros((pad_rows, QB), MXU_DTYPE)], axis=0)
            if not last:
                sbcar_scr[h] = carry + jnp.sum(run, axis=0, keepdims=True)

            def finish():
                pv = _dot(vt_of_head(h), a)
                acc = pv if first else sbacc_scr[h] + pv
                if last:
                    ya_ref[0, h] = acc.astype(ya_ref.dtype)
                else:
                    sbacc_scr[h] = acc

            return finish

        return logits, rest, n_j * SUBLANES, 0

    def sb_real_chunk(c, mask, first):
        r0 = pl.multiple_of(c * KC, KC)
        return sb_chunk(lambda pr: ka_ref[0, pr, pl.ds(r0, KC), :], lambda h: vat_ref[0, h, c],
                        mask, nslab, 0, first, False)

    w = wi_ref[0]
    lo = jnp.where(w >= 0.0, 0.0, -jnp.inf)
    hi = jnp.where(w >= 0.0, jnp.inf, 0.0)

    def score_keys(kic, mask):
        acc = jnp.zeros((kic.shape[0], QB), f32)
        for h in range(HEADS):
            s = _dot_nt(kic, qpad_scr[2, h])
            acc = acc + jnp.minimum(jnp.maximum(s, lo[h:h + 1, :]), hi[h:h + 1, :])
        if mask is not None:
            acc = jnp.where(mask, acc, -jnp.inf)
        bits = lax.bitcast_convert_type(acc, jnp.int32)
        key = jnp.where(bits < 0, bits ^ jnp.int32(0x7FFFFFFF), bits)
        return jnp.where(bits == jnp.int32(INT_MIN), 0, key)

    ones_word = jnp.full((SUBLANES, QB), -1, jnp.int32)

    def store_chunk_keys(c, r0, mask):
        key = score_keys(ki_ref[0, pl.ds(r0, KC), :], mask)
        key_scr[pl.ds(r0, KC), :] = key
        planes = _bit_planes(key)
        for o in range(32):
            plane_scr[c, o] = planes[o]
        alive_scr[c] = ones_word
        ones_scr[c] = ones_word

    def score_body(c, _):
        store_chunk_keys(c, pl.multiple_of(c * KC, KC), None)
        return 0

    lax.fori_loop(0, i, score_body, 0)
    store_chunk_keys(i, diag0, mask_le)
    meta_keys = score_keys(kim_ref[...], None)
    key_scr[seq:seq + N_META, :] = meta_keys

    def count_meta(cand, strict):
        cb = jnp.broadcast_to(cand, (N_META, QB))
        hit = (meta_keys > cb) if strict else (meta_keys >= cb)
        return jnp.sum(jnp.where(hit, 1.0, 0.0), axis=0, keepdims=True)

    def radix_step(it, st):
        ans, above, took = st
        cand = ans ^ lax.shift_left(jnp.int32(1), 31 - it)
        took_b = jnp.broadcast_to(took, (SUBLANES, QB)) != 0

        def body(c, acc):
            ones = ones_scr[c]
            alive = jnp.where(took_b, ones, alive_scr[c] ^ ones)
            ones = alive & plane_scr[c, it]
            alive_scr[c] = alive
            ones_scr[c] = ones
            return acc + lax.population_count(ones)

        acc = lax.fori_loop(0, i + 1, body, jnp.zeros((SUBLANES, QB), jnp.int32))
        ones_cnt = jnp.sum(acc.astype(f32), axis=0, keepdims=True)
        take = (above + ones_cnt + count_meta(cand, False)) >= float(topk)
        return (jnp.where(take, cand, ans), jnp.where(take, above, above + ones_cnt),
                jnp.where(take, -1, 0).astype(jnp.int32))

    thr, above, _ = lax.fori_loop(
        0, 32, radix_step,
        (jnp.full((1, QB), INT_MIN, jnp.int32), jnp.zeros((1, QB), f32), jnp.full((1, QB), -1, jnp.int32)))

    need = float(topk) - (above + count_meta(thr, True))
    thr_b = jnp.broadcast_to(thr, (SUBLANES, QB))
    need_b = jnp.broadcast_to(need, (SUBLANES, QB))

    def select_rows(blk, n_j, mask, carry):
        b3 = blk.reshape(n_j, SUBLANES, QB)
        tie = [jnp.where(b3[j] == thr_b, 1.0, 0.0) for j in range(n_j)]
        run = jnp.zeros((SUBLANES, QB), f32)
        prefix = []
        for j in range(n_j):
            run = run + tie[j]
            prefix.append(run)
        off = _group_excl_scan(run, reverse=False) + carry
        rows = []
        for j in range(n_j):
            rank = prefix[j] - tie[j] + off
            sel = (b3[j] > thr_b) | ((tie[j] > 0.0) & (rank < need_b))
            rows.append(jnp.where(sel, 0.0, NEG))
        mb = jnp.concatenate(rows, axis=0)
        if mask is not None:
            mb = jnp.where(mask, mb, NEG)
        return mb, carry + jnp.sum(run, axis=0, keepdims=True)

    mb, tie_carry = select_rows(key_scr[seq:seq + N_META, :], mslab, None, jnp.zeros((1, QB), f32))
    mb_scr[seq:seq + N_META, :] = mb

    def select_body(c, carry):
        r0 = pl.multiple_of(c * KC, KC)
        mb_c, carry = select_rows(key_scr[pl.ds(r0, KC), :], nslab, None, carry)
        mb_scr[pl.ds(r0, KC), :] = mb_c
        return carry

    tie_carry = lax.fori_loop(0, i, select_body, tie_carry)
    mb, _ = select_rows(key_scr[pl.ds(diag0, KC), :], nslab, mask_le, tie_carry)
    mb_scr[pl.ds(diag0, KC), :] = mb

    def dsa_chunk(kc, vt, mbias, bias_of_head, pad_rows, first, last):
        def logits(h):
            return _dot_nt(kc, qpad_scr[1, h])

        def rest(h, qk):
            lg = qk + mbias
            if bias_of_head is not None:
                lg = lg + bias_of_head(h)
            cmax = jnp.max(lg, axis=0, keepdims=True)
            if first:
                m_new = jnp.broadcast_to(cmax, (SUBLANES, QB))
            else:
                m_old = m_scr[h]
                m_new = jnp.maximum(m_old, cmax)
                alpha = jnp.exp2(m_old - m_new)
            p3 = lg.reshape(lg.shape[0] // SUBLANES, SUBLANES, QB)
            p = jnp.exp2(p3 - m_new[None]).reshape(lg.shape)
            psum = jnp.sum(p, axis=0, keepdims=True)
            pb = p.astype(MXU_DTYPE)
            if pad_rows:
                pb = jnp.concatenate([pb, jnp.zeros((pad_rows, QB), MXU_DTYPE)], axis=0)
            l_new = jnp.broadcast_to(psum, (SUBLANES, QB)) if first else l_scr[h] * alpha + psum
            if not last:
                m_scr[h] = m_new
                l_scr[h] = l_new

            def finish():
                pv = _dot(vt, pb)
                acc = pv if first else dacc_scr[h] * alpha[0:1, :] + pv
                if last:
                    yb_ref[0, h] = (acc / l_new[0:1, :]).astype(yb_ref.dtype)
                else:
                    dacc_scr[h] = acc

            return finish

        return logits, rest, kc.shape[0], 1

    def dsa_real_chunk(c, bias_of_head, first):
        r0 = pl.multiple_of(c * KC, KC)
        return dsa_chunk(kb_ref[0, pl.ds(r0, KC), :], vbt_ref[0, c], mb_scr[pl.ds(r0, KC), :],
                         bias_of_head, 0, first, False)

    run_streams(sb_real_chunk(i, mask_lt, True), dsa_real_chunk(i, lambda h: biasd_ref[h], True))

    def far_pair(t, _):
        run_streams(sb_real_chunk(i - 1 - t, None, False), dsa_real_chunk(t, None, False))
        return 0

    lax.fori_loop(0, i - 1, far_pair, 0)

    @pl.when(i > 0)
    def _():
        run_streams(sb_real_chunk(0, None, False), dsa_real_chunk(i - 1, lambda h: biasp_ref[h], False))

    first_block = (i == 0).astype(f32)
    run_streams(
        sb_chunk(lambda pr: kam_ref[pr], lambda h: vatm_ref[h], None, mslab, LANES - N_META, False, True),
        dsa_chunk(kbm_ref[...], vbtm_ref[...], mb_scr[seq:seq + N_META, :],
                  lambda h: biasm_ref[h] * first_block, LANES - N_META, False, True))


def _attn_call(qa, qb, qi, wit, ka, vat, ki, kb, vbt, kam, vatm, kim, kbm, vbtm,
               bias_d, bias_p, bias_m, topk):
    bsz, _, s, _ = qa.shape
    nq = s // QB
    nc = s // KC
    bf = MXU_DTYPE
    qspec = pl.BlockSpec((1, PAIRS, QB, LANES), lambda b, i: (b, 0, i, 0))
    const = lambda shape: pl.BlockSpec(shape, lambda b, i: (0,) * len(shape))
    yspec = pl.BlockSpec((1, HEADS, HEAD_DIM, QB), lambda b, i: (b, 0, 0, i))
    yshape = jax.ShapeDtypeStruct((bsz, HEADS, HEAD_DIM, s), bf)
    return pl.pallas_call(
        functools.partial(_attn_kernel, seq=s, topk=topk),
        grid=(bsz, nq),
        in_specs=[qspec, qspec, qspec,
                  pl.BlockSpec((1, HEADS, QB), lambda b, i: (b, 0, i)),
                  pl.BlockSpec((1, PAIRS, s, LANES), lambda b, i: (b, 0, 0, 0)),
                  pl.BlockSpec((1, HEADS, nc, HEAD_DIM, KC), lambda b, i: (b, 0, 0, 0, 0)),
                  pl.BlockSpec((1, s, LANES), lambda b, i: (b, 0, 0)),
                  pl.BlockSpec((1, s, LANES), lambda b, i: (b, 0, 0)),
                  pl.BlockSpec((1, nc, HEAD_DIM, KC), lambda b, i: (b, 0, 0, 0)),
                  const((PAIRS, N_META, LANES)), const((HEADS, HEAD_DIM, LANES)),
                  const((N_META, LANES)), const((N_META, LANES)), const((HEAD_DIM, LANES)),
                  const((HEADS, KC, QB)), const((HEADS, KC, QB)), const((HEADS, N_META, QB))],
        out_specs=[yspec, yspec],
        out_shape=[yshape, yshape],
        scratch_shapes=[pltpu.VMEM((3, HEADS, QB, LANES), bf),
                        pltpu.VMEM((s + N_META, QB), jnp.int32),
                        pltpu.VMEM((s + N_META, QB), jnp.float32),
                        pltpu.VMEM((HEADS, HEAD_DIM, QB), jnp.float32),
                        pltpu.VMEM((HEADS, SUBLANES, QB), jnp.float32),
                        pltpu.VMEM((HEADS, HEAD_DIM, QB), jnp.float32),
                        pltpu.VMEM((HEADS, SUBLANES, QB), jnp.float32),
                        pltpu.VMEM((HEADS, SUBLANES, QB), jnp.float32),
                        pltpu.VMEM((2, HEADS, KC, QB), jnp.float32),
                        pltpu.VMEM((nc, 32, SUBLANES, QB), jnp.int32),
                        pltpu.VMEM((nc, SUBLANES, QB), jnp.int32),
                        pltpu.VMEM((nc, SUBLANES, QB), jnp.int32)],
        compiler_params=pltpu.CompilerParams(
            dimension_semantics=("arbitrary", "arbitrary"), vmem_limit_bytes=VMEM_LIMIT),
        name="attn",
    )(qa, qb, qi, wit, ka, vat, ki, kb, vbt, kam, vatm, kim, kbm, vbtm, bias_d, bias_p, bias_m)


def _sigmoid(x):
    return 1.0 / (1.0 + jnp.exp(-x))


def _merge_kernel(x_ref, ya_ref, yb_ref, g_in_ref, b_in_ref, wz_ref, bg_ref, wpa_ref, wpb_ref,
                  wo_ref, g_ref, b_ref, o_ref):
    h = _layer_norm(x_ref[0], g_in_ref[...], b_in_ref[...])
    hb = h.astype(MXU_DTYPE)

    def branch(yt_ref, z_col, g_col, bg_col, wp_ref):
        z = _dot(hb, wz_ref[:, z_col:z_col + WIDTH])
        y = yt_ref[0].astype(jnp.float32).T
        y = (y * (z * _sigmoid(z))).astype(MXU_DTYPE)
        gate = _sigmoid(_dot(hb, wz_ref[:, g_col:g_col + D_MODEL]) + bg_ref[:, bg_col:bg_col + D_MODEL])
        return gate * _dot(y, wp_ref[...])

    merged = (branch(ya_ref, 0, 2 * WIDTH, 0, wpa_ref)
              + branch(yb_ref, WIDTH, 2 * WIDTH + D_MODEL, D_MODEL, wpb_ref))
    out = _dot(merged.astype(MXU_DTYPE), wo_ref[...])
    o_ref[0] = _layer_norm(DEEPNORM_ALPHA * h + out, g_ref[...], b_ref[...])


def _merge_call(x, ya, yb, ln_in_g, ln_in_b, wz, bg, wpa, wpb, wo, ln_g, ln_b, tm):
    bsz, s, d = x.shape
    assert s % tm == 0
    const = lambda shape: pl.BlockSpec(shape, lambda b, i: (0,) * len(shape))
    row = lambda width: pl.BlockSpec((1, tm, width), lambda b, i: (b, i, 0))
    col = pl.BlockSpec((1, WIDTH, tm), lambda b, i: (b, 0, i))
    return pl.pallas_call(
        _merge_kernel,
        grid=(bsz, s // tm),
        in_specs=[row(d), col, col, const((1, d)), const((1, d)),
                  const((d, 2 * WIDTH + 2 * d)), const((1, 2 * d)),
                  const((WIDTH, d)), const((WIDTH, d)), const((d, d)), const((1, d)), const((1, d))],
        out_specs=row(d),
        out_shape=jax.ShapeDtypeStruct((bsz, s, d), jnp.float32),
        compiler_params=pltpu.CompilerParams(
            dimension_semantics=("arbitrary", "arbitrary"), vmem_limit_bytes=VMEM_LIMIT),
        name="merge",
    )(x, ya, yb, ln_in_g, ln_in_b, wz, bg, wpa, wpb, wo, ln_g, ln_b)


def _perm_rows(a, axis, chunk):
    shp = a.shape
    n = shp[axis] // chunk
    a = a.reshape(shp[:axis] + (n, SUBLANES, chunk // SUBLANES) + shp[axis + 1:])
    a = jnp.swapaxes(a, axis + 1, axis + 2)
    return a.reshape(shp)


def _rel_bucket(dist):
    max_exact = REL_BUCKETS // 2
    nf = np.maximum(dist, 1).astype(np.float32)
    large = max_exact + (np.log(nf / np.float32(max_exact)) / np.float32(math.log(REL_MAX_DIST / max_exact))
                         * np.float32(REL_BUCKETS - max_exact)).astype(np.int32)
    large = np.minimum(large, REL_BUCKETS - 1)
    return np.where(dist < max_exact, dist, large)


def _bias_tiles(rel_bias):
    r = np.arange(KC)
    pos = (r % SUBLANES) * (KC // SUBLANES) + r // SUBLANES
    q = np.arange(QB)
    rm = np.arange(N_META)
    posm = (rm % SUBLANES) * (N_META // SUBLANES) + rm // SUBLANES
    table = (rel_bias - rel_bias[REL_BUCKETS - 1]).astype(jnp.float32) * LOG2E

    def tile(dist):
        onehot = jnp.asarray(_rel_bucket(dist)[..., None] == np.arange(REL_BUCKETS), jnp.float32)
        return jnp.einsum("rqb,bh->hrq", onehot, table, precision=lax.Precision.HIGHEST)

    return (tile(np.maximum(q[None, :] - pos[:, None], 0)),
            tile(KC + q[None, :] - pos[:, None]),
            tile(N_META + q[None, :] - posm[:, None]))


def kernel(x, meta_tokens, ln_in_g, ln_in_b, rel_bias, w_in, b_gate, idx_kn_g, idx_kn_b,
           w_pa, w_pb, w_o, ln_g, ln_b):
    bsz, seq, d = x.shape
    assert d == D_MODEL and seq % KC == 0 and w_in.shape[0] == DEPTH == 1
    topk = min(TOPK_MAX, seq // 4)
    nc = seq // KC
    bf = MXU_DTYPE
    w = w_in[0]
    dup = lambda a: jnp.concatenate([a, a], axis=-1)
    cols = lambda off, width: w[:, off:off + width]
    w_q = jnp.concatenate([cols(O_QA, WIDTH), cols(O_QB, WIDTH), cols(O_QI, WIDTH),
                           jnp.repeat(cols(O_WI, HEADS), HEAD_DIM, axis=1)], axis=1).astype(bf)
    w_k = jnp.concatenate([cols(O_KA, WIDTH), dup(cols(O_KI, HEAD_DIM)), dup(cols(O_KB, HEAD_DIM))],
                          axis=1).astype(bf)
    w_vt = jnp.concatenate([cols(O_VA, WIDTH), cols(O_VB, HEAD_DIM)], axis=1).T.astype(bf)
    w_wit = jnp.pad(cols(O_WI, HEADS).T, ((0, WI_ROWS - HEADS), (0, 0))).astype(bf)
    w_z = jnp.concatenate([cols(O_ZA, WIDTH), cols(O_ZB, WIDTH), cols(O_GA, D_MODEL), cols(O_GB, D_MODEL)],
                          axis=1).astype(bf)
    row = lambda a: a.reshape(1, -1).astype(jnp.float32)
    r = np.arange(KC)
    perm = np.zeros((KC, KC), np.float32)
    perm[r, (r % SUBLANES) * (KC // SUBLANES) + r // SUBLANES] = 1.0

    proj = functools.partial(_proj_call, ln_g=row(ln_in_g), ln_b=row(ln_in_b), wq=w_q, wk=w_k, wvt=w_vt,
                             wit=w_wit, kn_g2=row(dup(idx_kn_g[0])), kn_b2=row(dup(idx_kn_b[0])))
    qa, qb, qi, wit, ka, ki, kb, vat, vbt = proj(x, perm=jnp.asarray(perm, bf))
    x_meta = jnp.pad(meta_tokens.astype(x.dtype), ((0, KC - N_META), (0, 0)))[None]
    _, _, _, _, kam, kim, kbm, vatm, vbtm = proj(x_meta, perm=jnp.eye(KC, dtype=bf))
    pad_keys = lambda a: jnp.pad(a, [(0, 0)] * (a.ndim - 1) + [(0, LANES - N_META)])
    kam = _perm_rows(kam[0, :, :N_META], 1, N_META)
    kim = _perm_rows(kim[0, :N_META], 0, N_META)
    kbm = _perm_rows(kbm[0, :N_META], 0, N_META)
    vatm = pad_keys(_perm_rows(vatm[0, :, 0, :, :N_META], 2, N_META))
    vbtm = pad_keys(_perm_rows(vbtm[0, 0, :, :N_META], 1, N_META))
    bias_d, bias_p, bias_m = _bias_tiles(rel_bias)

    yat, ybt = _attn_call(qa, qb, qi, wit, ka, vat, ki, kb, vbt, kam, vatm, kim, kbm, vbtm,
                          bias_d, bias_p, bias_m, topk)
    ya = yat.reshape(bsz, WIDTH, seq)
    yb = ybt.reshape(bsz, WIDTH, seq)

    return _merge_call(x, ya, yb, row(ln_in_g), row(ln_in_b), w_z, row(b_gate[0]),
                       w_pa[0].astype(bf), w_pb[0].astype(bf), w_o[0].astype(bf),
                       row(ln_g[0]), row(ln_b[0]), tm=min(256, seq))
```

```python
import functools
import math

import jax
import jax.numpy as jnp
import numpy as np
from jax import lax
from jax.experimental import pallas as pl
from jax.experimental.pallas import tpu as pltpu

D_MODEL = 1024
DEPTH = 1
N_META = 16
HEADS = 8
HEAD_DIM = 64
WIDTH = HEADS * HEAD_DIM
TOPK_MAX = 256
REL_BUCKETS = 32
REL_MAX_DIST = 128
LN_EPS = 1e-5
DEEPNORM_ALPHA = (2.0 * DEPTH) ** 0.25
IDX_SCALE = HEADS ** -0.5 * HEAD_DIM ** -0.5
QK_SCALE = HEAD_DIM ** -0.5
LOG2E = math.log2(math.e)
QK_SCALE_LOG2 = QK_SCALE * LOG2E

SUBLANES = 8
LANES = 128
QB = 256
KC = 256
PAIRS = HEADS // 2
PROJ_CHUNKS = 2
MERGE_CHUNKS = 4
MERGE_ROWS = 256
DEN_ROWS = 16
NEG = -1e30
SB_DONE_BITS = 160.0
INT_MIN = -2 ** 31
VMEM_LIMIT = 56 * 1024 * 1024
MXU_DTYPE = jnp.bfloat16

_COLS = [WIDTH, WIDTH, WIDTH, WIDTH, WIDTH, HEAD_DIM, HEAD_DIM, WIDTH, WIDTH, HEAD_DIM, HEADS,
         D_MODEL, D_MODEL]
_OFF = [0]
for _c in _COLS:
    _OFF.append(_OFF[-1] + _c)
(O_QA, O_KA, O_VA, O_ZA, O_QB, O_KB, O_VB, O_ZB, O_QI, O_KI, O_WI, O_GA, O_GB, O_END) = _OFF


def _layer_norm(x, g, b):
    mu = jnp.mean(x, axis=-1, keepdims=True)
    xc = x - mu
    var = jnp.mean(xc * xc, axis=-1, keepdims=True)
    return xc * lax.rsqrt(var + LN_EPS) * g + b


def _dot(a, b):
    return jnp.dot(a, b, preferred_element_type=jnp.float32)


def _dot_nt(a, b):
    return lax.dot_general(a, b, (((1,), (1,)), ((), ())), preferred_element_type=jnp.float32)


WQ_COLS = 4 * WIDTH
WK_COLS = WIDTH + 2 * LANES
WV_ROWS = WIDTH + HEAD_DIM
WI_ROWS = 2 * SUBLANES


def _proj_kernel(x_ref, g_ref, b_ref, perm_ref, wq_ref, wk_ref, wvt_ref, wit_ref, kng_ref, knb_ref,
                 qa_ref, qb_ref, qi_ref, wi_ref, ka_ref, ki_ref, kb_ref, vat_ref, vbt_ref):
    for c in range(x_ref.shape[1] // KC):
        rows = slice(c * KC, (c + 1) * KC)
        h = _layer_norm(x_ref[0, rows], g_ref[...], b_ref[...]).astype(MXU_DTYPE)
        hp = _dot(perm_ref[...], h).astype(MXU_DTYPE)

        def store_pairs(ref, val):
            for pr in range(PAIRS):
                ref[0, pr, rows] = val[:, pr * LANES:(pr + 1) * LANES].astype(ref.dtype)

        qcols = lambda k: _dot(h, wq_ref[:, k * WIDTH:(k + 1) * WIDTH])
        store_pairs(qa_ref, qcols(0) * QK_SCALE_LOG2)
        store_pairs(qb_ref, qcols(1) * QK_SCALE_LOG2)
        store_pairs(qi_ref, qcols(2) * qcols(3) * IDX_SCALE)
        wi_ref[0, :, rows] = _dot_nt(wit_ref[...], h)[0:HEADS, :]

        store_pairs(ka_ref, _dot(hp, wk_ref[:, 0:WIDTH]))
        ki_ref[0, rows] = _layer_norm(_dot(hp, wk_ref[:, WIDTH:WIDTH + LANES]),
                                      kng_ref[...], knb_ref[...]).astype(ki_ref.dtype)
        kb_ref[0, rows] = _dot(hp, wk_ref[:, WIDTH + LANES:WK_COLS]).astype(kb_ref.dtype)
        vat = _dot_nt(wvt_ref[0:WIDTH, :], hp)
        vat_ref[0, :, c] = vat.reshape(HEADS, HEAD_DIM, KC).astype(vat_ref.dtype)
        vbt_ref[0, c] = _dot_nt(wvt_ref[WIDTH:WV_ROWS, :], hp).astype(vbt_ref.dtype)


def _proj_call(x, ln_g, ln_b, perm, wq, wk, wvt, wit, kn_g2, kn_b2, step_chunks):
    bsz, s, d = x.shape
    rows = step_chunks * KC
    assert s % rows == 0 and perm.shape == (KC, KC)
    nc = s // KC
    bf = MXU_DTYPE
    pair_shape = jax.ShapeDtypeStruct((bsz, PAIRS, s, LANES), bf)
    pair_spec = pl.BlockSpec((1, PAIRS, rows, LANES), lambda b, i: (b, 0, i, 0))
    row128 = pl.BlockSpec((1, rows, LANES), lambda b, i: (b, i, 0))
    row_shape = jax.ShapeDtypeStruct((bsz, s, LANES), bf)
    const = lambda shape: pl.BlockSpec(shape, lambda b, i: (0,) * len(shape))
    return pl.pallas_call(
        _proj_kernel,
        grid=(bsz, s // rows),
        in_specs=[pl.BlockSpec((1, rows, d), lambda b, i: (b, i, 0)),
                  const((1, d)), const((1, d)), const((KC, KC)),
                  const((d, WQ_COLS)), const((d, WK_COLS)), const((WV_ROWS, d)), const((WI_ROWS, d)),
                  const((1, LANES)), const((1, LANES))],
        out_specs=[pair_spec, pair_spec, pair_spec,
                   pl.BlockSpec((1, HEADS, rows), lambda b, i: (b, 0, i)),
                   pair_spec, row128, row128,
                   pl.BlockSpec((1, HEADS, rows // KC, HEAD_DIM, KC), lambda b, i: (b, 0, i, 0, 0)),
                   pl.BlockSpec((1, rows // KC, HEAD_DIM, KC), lambda b, i: (b, i, 0, 0))],
        out_shape=[pair_shape, pair_shape, pair_shape,
                   jax.ShapeDtypeStruct((bsz, HEADS, s), jnp.float32),
                   pair_shape, row_shape, row_shape,
                   jax.ShapeDtypeStruct((bsz, HEADS, nc, HEAD_DIM, KC), bf),
                   jax.ShapeDtypeStruct((bsz, nc, HEAD_DIM, KC), bf)],
        compiler_params=pltpu.CompilerParams(
            dimension_semantics=("arbitrary", "arbitrary"), vmem_limit_bytes=VMEM_LIMIT),
        name="proj",
    )(x, ln_g, ln_b, perm, wq, wk, wvt, wit, kn_g2, kn_b2)


def _group_excl_scan(g_tot, reverse):
    sub = lax.broadcasted_iota(jnp.int32, g_tot.shape, 0)
    out = jnp.zeros_like(g_tot)
    for g in range(SUBLANES):
        row = g_tot[g:g + 1, :]
        take = (sub < g) if reverse else (sub > g)
        out = out + jnp.where(take, row, 0.0)
    return out


def _bit_planes(key):
    assert key.shape[0] == 32 * SUBLANES
    k3 = key.reshape(32, SUBLANES, key.shape[1])
    a = [k3[j] ^ jnp.int32(INT_MIN) for j in range(32)]
    as_i32 = lambda m: jnp.int32(m - (1 << 32) if m >= (1 << 31) else m)
    j, m = 16, 0x0000FFFF
    while j:
        shift = jnp.full(a[0].shape, j, jnp.int32)
        k = 0
        while k < 32:
            t = (a[k] ^ lax.shift_right_logical(a[k + j], shift)) & as_i32(m)
            a[k] = a[k] ^ t
            a[k + j] = a[k + j] ^ lax.shift_left(t, shift)
            k = (k + j + 1) & ~j
        j >>= 1
        m = (m ^ (m << j)) & 0xFFFFFFFF
    return a


def _attn_kernel(qa_ref, qb_ref, qi_ref, wi_ref,
                 ka_ref, vat_ref, ki_ref, kb_ref, vbt_ref,
                 kam_ref, vatm_ref, kim_ref, kbm_ref, vbtm_ref,
                 biasd_ref, biasp_ref, biasm_ref,
                 ya_ref, yb_ref,
                 qpad_scr, key_scr, mb_scr, sbacc_scr, sbcar_scr, dacc_scr, m_scr, z_scr,
                 plane_scr, alive_scr, ones_scr, *, seq, topk):
    i = pl.program_id(1)
    nslab = KC // SUBLANES
    mslab = N_META // SUBLANES
    f32 = jnp.float32

    lane = lax.broadcasted_iota(jnp.int32, (QB, LANES), 1)
    lo_half = (lane < HEAD_DIM).astype(f32).astype(MXU_DTYPE)
    hi_half = (lane >= HEAD_DIM).astype(f32).astype(MXU_DTYPE)
    for t, ref in enumerate((qa_ref, qb_ref, qi_ref)):
        for pr in range(PAIRS):
            qp = ref[0, pr]
            qpad_scr[t, 2 * pr] = qp * lo_half
            qpad_scr[t, 2 * pr + 1] = qp * hi_half

    r_io = lax.broadcasted_iota(jnp.int32, (KC, QB), 0)
    c_io = lax.broadcasted_iota(jnp.int32, (KC, QB), 1)
    pos = (r_io & (SUBLANES - 1)) * nslab + (r_io >> 3)
    mask_lt = pos < c_io
    mask_le = pos <= c_io

    diag0 = pl.multiple_of(i * KC, KC)

    def issue(*streams):
        for logits_of_head, _, rows, branch in streams:
            for h in range(HEADS):
                z_scr[branch, h, 0:rows] = logits_of_head(h)

    def finish(*streams):
        for h in range(HEADS):
            for _, rest_of_head, rows, branch in streams:
                rest_of_head(h, z_scr[branch, h, 0:rows])()

    def sb_chunk(k_of_pair, vt_of_head, mask, n_j, pad_rows, first):
        def logits(h):
            return _dot_nt(k_of_pair(h // 2), qpad_scr[0, h])

        def rest(h, z):
            neg_abs = lax.bitcast_convert_type(
                lax.bitcast_convert_type(z, jnp.int32) | jnp.int32(INT_MIN), f32)
            sp = jnp.maximum(z, 0.0) + jnp.log2(1.0 + jnp.exp2(neg_abs))
            if mask is not None:
                sp = jnp.where(mask, sp, 0.0)
            sp3 = sp.reshape(n_j, SUBLANES, QB)
            z3 = z.reshape(n_j, SUBLANES, QB)
            run = jnp.zeros((SUBLANES, QB), f32)
            u = [None] * n_j
            for j in reversed(range(n_j)):
                run = run + sp3[j]
                u[j] = jnp.exp2(z3[j] - run)
            carry = jnp.zeros((SUBLANES, QB), f32) if first else sbcar_scr[h]
            scale = jnp.exp2(-(_group_excl_scan(run, reverse=True) + carry))
            a = jnp.concatenate([u[j] * scale for j in range(n_j)], axis=0)
            if mask is not None:
                a = jnp.where(mask, a, 0.0)
            a = a.astype(MXU_DTYPE)
            if pad_rows:
                a = jnp.concatenate([a, jnp.zeros((pad_rows, QB), MXU_DTYPE)], axis=0)
            sbcar_scr[h] = carry + jnp.sum(run, axis=0, keepdims=True)

            def finish():
                pv = _dot(vt_of_head(h), a)
                sbacc_scr[h] = pv if first else sbacc_scr[h] + pv

            return finish

        return logits, rest, n_j * SUBLANES, 0

    def sb_real_chunk(c, mask, first):
        r0 = pl.multiple_of(c * KC, KC)
        return sb_chunk(lambda pr: ka_ref[0, pr, pl.ds(r0, KC), :], lambda h: vat_ref[0, h, c],
                        mask, nslab, 0, first)

    w = wi_ref[0]
    lo = jnp.where(w >= 0.0, 0.0, -jnp.inf)
    hi = jnp.where(w >= 0.0, jnp.inf, 0.0)

    def score_keys(kic, mask):
        acc = jnp.zeros((kic.shape[0], QB), f32)
        for h in range(HEADS):
            s = _dot_nt(kic, qpad_scr[2, h])
            acc = acc + jnp.minimum(jnp.maximum(s, lo[h:h + 1, :]), hi[h:h + 1, :])
        if mask is not None:
            acc = jnp.where(mask, acc, -jnp.inf)
        bits = lax.bitcast_convert_type(acc, jnp.int32)
        key = jnp.where(bits < 0, bits ^ jnp.int32(0x7FFFFFFF), bits)
        return jnp.where(bits == jnp.int32(INT_MIN), 0, key)

    ones_word = jnp.full((SUBLANES, QB), -1, jnp.int32)

    def store_chunk_keys(c, r0, mask):
        key = score_keys(ki_ref[0, pl.ds(r0, KC), :], mask)
        key_scr[pl.ds(r0, KC), :] = key
        planes = _bit_planes(key)
        for o in range(32):
            plane_scr[c, o] = planes[o]
        alive_scr[c] = ones_word
        ones_scr[c] = ones_word

    def score_body(c, _):
        store_chunk_keys(c, pl.multiple_of(c * KC, KC), None)
        return 0

    lax.fori_loop(0, i, score_body, 0)
    store_chunk_keys(i, diag0, mask_le)
    meta_keys = score_keys(kim_ref[...], None)
    key_scr[seq:seq + N_META, :] = meta_keys

    def count_meta(cand, strict):
        cb = jnp.broadcast_to(cand, (N_META, QB))
        hit = (meta_keys > cb) if strict else (meta_keys >= cb)
        return jnp.sum(jnp.where(hit, 1.0, 0.0), axis=0, keepdims=True)

    def radix_step(it, st):
        ans, above, took = st
        cand = ans ^ lax.shift_left(jnp.int32(1), 31 - it)
        took_b = jnp.broadcast_to(took, (SUBLANES, QB)) != 0

        def body(c, acc):
            ones = ones_scr[c]
            alive = jnp.where(took_b, ones, alive_scr[c] ^ ones)
            ones = alive & plane_scr[c, it]
            alive_scr[c] = alive
            ones_scr[c] = ones
            return acc + lax.population_count(ones)

        acc = lax.fori_loop(0, i + 1, body, jnp.zeros((SUBLANES, QB), jnp.int32))
        ones_cnt = jnp.sum(acc.astype(f32), axis=0, keepdims=True)
        take = (above + ones_cnt + count_meta(cand, False)) >= float(topk)
        return (jnp.where(take, cand, ans), jnp.where(take, above, above + ones_cnt),
                jnp.where(take, -1, 0).astype(jnp.int32))

    thr, above, _ = lax.fori_loop(
        0, 32, radix_step,
        (jnp.full((1, QB), INT_MIN, jnp.int32), jnp.zeros((1, QB), f32), jnp.full((1, QB), -1, jnp.int32)))

    need = float(topk) - (above + count_meta(thr, True))
    thr_b = jnp.broadcast_to(thr, (SUBLANES, QB))
    need_b = jnp.broadcast_to(need, (SUBLANES, QB))

    def select_rows(blk, n_j, mask, carry):
        b3 = blk.reshape(n_j, SUBLANES, QB)
        tie = [jnp.where(b3[j] == thr_b, 1.0, 0.0) for j in range(n_j)]
        run = jnp.zeros((SUBLANES, QB), f32)
        prefix = []
        for j in range(n_j):
            run = run + tie[j]
            prefix.append(run)
        off = _group_excl_scan(run, reverse=False) + carry
        rows = []
        for j in range(n_j):
            rank = prefix[j] - tie[j] + off
            sel = (b3[j] > thr_b) | ((tie[j] > 0.0) & (rank < need_b))
            rows.append(jnp.where(sel, 0.0, NEG))
        mb = jnp.concatenate(rows, axis=0)
        if mask is not None:
            mb = jnp.where(mask, mb, NEG)
        return mb, carry + jnp.sum(run, axis=0, keepdims=True)

    mb, tie_carry = select_rows(key_scr[seq:seq + N_META, :], mslab, None, jnp.zeros((1, QB), f32))
    mb_scr[seq:seq + N_META, :] = mb

    def select_body(c, carry):
        r0 = pl.multiple_of(c * KC, KC)
        mb_c, carry = select_rows(key_scr[pl.ds(r0, KC), :], nslab, None, carry)
        mb_scr[pl.ds(r0, KC), :] = mb_c
        return carry

    tie_carry = lax.fori_loop(0, i, select_body, tie_carry)
    mb, _ = select_rows(key_scr[pl.ds(diag0, KC), :], nslab, mask_le, tie_carry)
    mb_scr[pl.ds(diag0, KC), :] = mb

    def dsa_chunk(kc_fn, vt_fn, mbias_fn, bias_of_head, rows, pad_rows, first, last):
        def logits(h):
            return _dot_nt(kc_fn(), qpad_scr[1, h])

        def rest(h, qk):
            lg = qk + mbias_fn()
            if bias_of_head is not None:
                lg = lg + bias_of_head(h)
            cmax = jnp.max(lg, axis=0, keepdims=True)
            if first:
                m_new = jnp.broadcast_to(cmax, (SUBLANES, QB))
            else:
                m_old = m_scr[h]
                m_new = jnp.maximum(m_old, cmax)
                alpha = jnp.exp2(m_old - m_new)
            p3 = lg.reshape(lg.shape[0] // SUBLANES, SUBLANES, QB)
            pb = jnp.exp2(p3 - m_new[None]).reshape(lg.shape).astype(MXU_DTYPE)
            if pad_rows:
                pb = jnp.concatenate([pb, jnp.zeros((pad_rows, QB), MXU_DTYPE)], axis=0)
            if not last:
                m_scr[h] = m_new

            def finish():
                vt = vt_fn()
                vt_ones = jnp.concatenate([vt, jnp.ones((DEN_ROWS, vt.shape[1]), vt.dtype)], axis=0)
                pv = _dot(vt_ones, pb)
                acc = pv if first else dacc_scr[h] * alpha[0:1, :] + pv
                if last:
                    yb_ref[0, h] = (acc[0:HEAD_DIM] / acc[HEAD_DIM:HEAD_DIM + 1]).astype(yb_ref.dtype)
                else:
                    dacc_scr[h] = acc

            return finish

        return logits, rest, rows, 1

    def dsa_real_chunk(c, bias_of_head, first):
        r0 = pl.multiple_of(c * KC, KC)
        return dsa_chunk(lambda: kb_ref[0, pl.ds(r0, KC), :], lambda: vbt_ref[0, c],
                         lambda: mb_scr[pl.ds(r0, KC), :], bias_of_head, KC, 0, first, False)

    def run_pair(*streams):
        issue(*streams)
        finish(*streams)

    def pair(p, bias_of_head=None):
        return sb_real_chunk(i - 1 - p, None, False), dsa_real_chunk(p, bias_of_head, False)

    def sb_open():
        return jnp.min(jnp.min(sbcar_scr[...], axis=0)) < SB_DONE_BITS

    run_pair(sb_real_chunk(i, mask_lt, True), dsa_real_chunk(i, lambda h: biasd_ref[h], True))

    def both_cond(st):
        p, still_open = st
        return (p < i - 1) & still_open

    def both_body(st):
        run_pair(*pair(st[0]))
        return st[0] + 1, sb_open()

    p_sparse, _ = lax.while_loop(both_cond, both_body, (jnp.int32(0), sb_open()))

    def sparse_only(p, _):
        run_pair(dsa_real_chunk(p, None, False))
        return 0

    lax.fori_loop(p_sparse, i - 1, sparse_only, 0)

    prev_bias = lambda h: biasp_ref[h]
    open_prev = sb_open()

    @pl.when((i > 0) & open_prev)
    def _():
        run_pair(*pair(i - 1, prev_bias))

    @pl.when((i > 0) & jnp.logical_not(open_prev))
    def _():
        run_pair(dsa_real_chunk(i - 1, prev_bias, False))

    first_block = (i == 0).astype(f32)
    sb_meta = sb_chunk(lambda pr: kam_ref[pr], lambda h: vatm_ref[h], None, mslab, LANES - N_META, False)
    dsa_meta = dsa_chunk(lambda: kbm_ref[...], lambda: vbtm_ref[...], lambda: mb_scr[seq:seq + N_META, :],
                         lambda h: biasm_ref[h] * first_block, N_META, LANES - N_META, False, True)
    open_meta = sb_open()

    @pl.when(open_meta)
    def _():
        run_pair(sb_meta, dsa_meta)

    @pl.when(jnp.logical_not(open_meta))
    def _():
        run_pair(dsa_meta)

    for h in range(HEADS):
        ya_ref[0, h] = sbacc_scr[h].astype(ya_ref.dtype)


def _attn_call(qa, qb, qi, wit, ka, vat, ki, kb, vbt, kam, vatm, kim, kbm, vbtm,
               bias_d, bias_p, bias_m, topk):
    bsz, _, s, _ = qa.shape
    nq = s // QB
    nc = s // KC
    bf = MXU_DTYPE
    qspec = pl.BlockSpec((1, PAIRS, QB, LANES), lambda b, i: (b, 0, i, 0))
    const = lambda shape: pl.BlockSpec(shape, lambda b, i: (0,) * len(shape))
    yspec = pl.BlockSpec((1, HEADS, HEAD_DIM, QB), lambda b, i: (b, 0, 0, i))
    yshape = jax.ShapeDtypeStruct((bsz, HEADS, HEAD_DIM, s), bf)
    return pl.pallas_call(
        functools.partial(_attn_kernel, seq=s, topk=topk),
        grid=(bsz, nq),
        in_specs=[qspec, qspec, qspec,
                  pl.BlockSpec((1, HEADS, QB), lambda b, i: (b, 0, i)),
                  pl.BlockSpec((1, PAIRS, s, LANES), lambda b, i: (b, 0, 0, 0)),
                  pl.BlockSpec((1, HEADS, nc, HEAD_DIM, KC), lambda b, i: (b, 0, 0, 0, 0)),
                  pl.BlockSpec((1, s, LANES), lambda b, i: (b, 0, 0)),
                  pl.BlockSpec((1, s, LANES), lambda b, i: (b, 0, 0)),
                  pl.BlockSpec((1, nc, HEAD_DIM, KC), lambda b, i: (b, 0, 0, 0)),
                  const((PAIRS, N_META, LANES)), const((HEADS, HEAD_DIM, LANES)),
                  const((N_META, LANES)), const((N_META, LANES)), const((HEAD_DIM, LANES)),
                  const((HEADS, KC, QB)), const((HEADS, KC, QB)), const((HEADS, N_META, QB))],
        out_specs=[yspec, yspec],
        out_shape=[yshape, yshape],
        scratch_shapes=[pltpu.VMEM((3, HEADS, QB, LANES), bf),
                        pltpu.VMEM((s + N_META, QB), jnp.int32),
                        pltpu.VMEM((s + N_META, QB), jnp.float32),
                        pltpu.VMEM((HEADS, HEAD_DIM, QB), jnp.float32),
                        pltpu.VMEM((HEADS, SUBLANES, QB), jnp.float32),
                        pltpu.VMEM((HEADS, HEAD_DIM + DEN_ROWS, QB), jnp.float32),
                        pltpu.VMEM((HEADS, SUBLANES, QB), jnp.float32),
                        pltpu.VMEM((2, HEADS, KC, QB), jnp.float32),
                        pltpu.VMEM((nc, 32, SUBLANES, QB), jnp.int32),
                        pltpu.VMEM((nc, SUBLANES, QB), jnp.int32),
                        pltpu.VMEM((nc, SUBLANES, QB), jnp.int32)],
        compiler_params=pltpu.CompilerParams(
            dimension_semantics=("arbitrary", "arbitrary"), vmem_limit_bytes=VMEM_LIMIT),
        name="attn",
    )(qa, qb, qi, wit, ka, vat, ki, kb, vbt, kam, vatm, kim, kbm, vbtm, bias_d, bias_p, bias_m)


def _sigmoid(x):
    return 1.0 / (1.0 + jnp.exp(-x))


def _merge_kernel(x_ref, ya_ref, yb_ref, g_in_ref, b_in_ref, wz_ref, bg_ref, wpa_ref, wpb_ref,
                  wo_ref, g_ref, b_ref, o_ref):
    for t in range(x_ref.shape[1] // MERGE_ROWS):
        rows = slice(t * MERGE_ROWS, (t + 1) * MERGE_ROWS)
        h = _layer_norm(x_ref[0, rows], g_in_ref[...], b_in_ref[...])
        hb = h.astype(MXU_DTYPE)

        def branch(yt_ref, z_col, g_col, bg_col, wp_ref):
            z = _dot(hb, wz_ref[:, z_col:z_col + WIDTH])
            y = yt_ref[0, :, rows].astype(jnp.float32).T
            y = (y * (z * _sigmoid(z))).astype(MXU_DTYPE)
            gate = _sigmoid(_dot(hb, wz_ref[:, g_col:g_col + D_MODEL]) + bg_ref[:, bg_col:bg_col + D_MODEL])
            return gate * _dot(y, wp_ref[...])

        merged = (branch(ya_ref, 0, 2 * WIDTH, 0, wpa_ref)
                  + branch(yb_ref, WIDTH, 2 * WIDTH + D_MODEL, D_MODEL, wpb_ref))
        out = _dot(merged.astype(MXU_DTYPE), wo_ref[...])
        o_ref[0, rows] = _layer_norm(DEEPNORM_ALPHA * h + out, g_ref[...], b_ref[...])


def _merge_call(x, ya, yb, ln_in_g, ln_in_b, wz, bg, wpa, wpb, wo, ln_g, ln_b, tm):
    bsz, s, d = x.shape
    assert s % tm == 0
    const = lambda shape: pl.BlockSpec(shape, lambda b, i: (0,) * len(shape))
    row = lambda width: pl.BlockSpec((1, tm, width), lambda b, i: (b, i, 0))
    col = pl.BlockSpec((1, WIDTH, tm), lambda b, i: (b, 0, i))
    return pl.pallas_call(
        _merge_kernel,
        grid=(bsz, s // tm),
        in_specs=[row(d), col, col, const((1, d)), const((1, d)),
                  const((d, 2 * WIDTH + 2 * d)), const((1, 2 * d)),
                  const((WIDTH, d)), const((WIDTH, d)), const((d, d)), const((1, d)), const((1, d))],
        out_specs=row(d),
        out_shape=jax.ShapeDtypeStruct((bsz, s, d), jnp.float32),
        compiler_params=pltpu.CompilerParams(
            dimension_semantics=("arbitrary", "arbitrary"), vmem_limit_bytes=VMEM_LIMIT),
        name="merge",
    )(x, ya, yb, ln_in_g, ln_in_b, wz, bg, wpa, wpb, wo, ln_g, ln_b)


def _perm_rows(a, axis, chunk):
    shp = a.shape
    n = shp[axis] // chunk
    a = a.reshape(shp[:axis] + (n, SUBLANES, chunk // SUBLANES) + shp[axis + 1:])
    a = jnp.swapaxes(a, axis + 1, axis + 2)
    return a.reshape(shp)


def _rel_bucket(dist):
    max_exact = REL_BUCKETS // 2
    nf = np.maximum(dist, 1).astype(np.float32)
    large = max_exact + (np.log(nf / np.float32(max_exact)) / np.float32(math.log(REL_MAX_DIST / max_exact))
                         * np.float32(REL_BUCKETS - max_exact)).astype(np.int32)
    large = np.minimum(large, REL_BUCKETS - 1)
    return np.where(dist < max_exact, dist, large)


def _bias_tiles(rel_bias):
    r = np.arange(KC)
    pos = (r % SUBLANES) * (KC // SUBLANES) + r // SUBLANES
    q = np.arange(QB)
    rm = np.arange(N_META)
    posm = (rm % SUBLANES) * (N_META // SUBLANES) + rm // SUBLANES
    table = (rel_bias - rel_bias[REL_BUCKETS - 1]).astype(jnp.float32) * LOG2E

    def tile(dist):
        onehot = jnp.asarray(_rel_bucket(dist)[..., None] == np.arange(REL_BUCKETS), jnp.float32)
        return jnp.einsum("rqb,bh->hrq", onehot, table, precision=lax.Precision.HIGHEST)

    return (tile(np.maximum(q[None, :] - pos[:, None], 0)),
            tile(KC + q[None, :] - pos[:, None]),
            tile(N_META + q[None, :] - posm[:, None]))


def kernel(x, meta_tokens, ln_in_g, ln_in_b, rel_bias, w_in, b_gate, idx_kn_g, idx_kn_b,
           w_pa, w_pb, w_o, ln_g, ln_b):
    bsz, seq, d = x.shape
    assert d == D_MODEL and seq % KC == 0 and w_in.shape[0] == DEPTH == 1
    topk = min(TOPK_MAX, seq // 4)
    nc = seq // KC
    bf = MXU_DTYPE
    w = w_in[0]
    dup = lambda a: jnp.concatenate([a, a], axis=-1)
    cols = lambda off, width: w[:, off:off + width]
    w_q = jnp.concatenate([cols(O_QA, WIDTH), cols(O_QB, WIDTH), cols(O_QI, WIDTH),
                           jnp.repeat(cols(O_WI, HEADS), HEAD_DIM, axis=1)], axis=1).astype(bf)
    w_k = jnp.concatenate([cols(O_KA, WIDTH), dup(cols(O_KI, HEAD_DIM)), dup(cols(O_KB, HEAD_DIM))],
                          axis=1).astype(bf)
    w_vt = jnp.concatenate([cols(O_VA, WIDTH), cols(O_VB, HEAD_DIM)], axis=1).T.astype(bf)
    w_wit = jnp.pad(cols(O_WI, HEADS).T, ((0, WI_ROWS - HEADS), (0, 0))).astype(bf)
    w_z = jnp.concatenate([cols(O_ZA, WIDTH), cols(O_ZB, WIDTH), cols(O_GA, D_MODEL), cols(O_GB, D_MODEL)],
                          axis=1).astype(bf)
    row = lambda a: a.reshape(1, -1).astype(jnp.float32)
    r = np.arange(KC)
    perm = np.zeros((KC, KC), np.float32)
    perm[r, (r % SUBLANES) * (KC // SUBLANES) + r // SUBLANES] = 1.0

    proj = functools.partial(_proj_call, ln_g=row(ln_in_g), ln_b=row(ln_in_b), wq=w_q, wk=w_k, wvt=w_vt,
                             wit=w_wit, kn_g2=row(dup(idx_kn_g[0])), kn_b2=row(dup(idx_kn_b[0])))
    qa, qb, qi, wit, ka, ki, kb, vat, vbt = proj(x, perm=jnp.asarray(perm, bf),
                                                 step_chunks=math.gcd(nc, PROJ_CHUNKS))
    x_meta = jnp.pad(meta_tokens.astype(x.dtype), ((0, KC - N_META), (0, 0)))[None]
    _, _, _, _, kam, kim, kbm, vatm, vbtm = proj(x_meta, perm=jnp.eye(KC, dtype=bf), step_chunks=1)
    pad_keys = lambda a: jnp.pad(a, [(0, 0)] * (a.ndim - 1) + [(0, LANES - N_META)])
    kam = _perm_rows(kam[0, :, :N_META], 1, N_META)
    kim = _perm_rows(kim[0, :N_META], 0, N_META)
    kbm = _perm_rows(kbm[0, :N_META], 0, N_META)
    vatm = pad_keys(_perm_rows(vatm[0, :, 0, :, :N_META], 2, N_META))
    vbtm = pad_keys(_perm_rows(vbtm[0, 0, :, :N_META], 1, N_META))
    bias_d, bias_p, bias_m = _bias_tiles(rel_bias)

    yat, ybt = _attn_call(qa, qb, qi, wit, ka, vat, ki, kb, vbt, kam, vatm, kim, kbm, vbtm,
                          bias_d, bias_p, bias_m, topk)
    ya = yat.reshape(bsz, WIDTH, seq)
    yb = ybt.reshape(bsz, WIDTH, seq)

    return _merge_call(x, ya, yb, row(ln_in_g), row(ln_in_b), w_z, row(b_gate[0]),
                       w_pa[0].astype(bf), w_pb[0].astype(bf), w_o[0].astype(bf),
                       row(ln_g[0]), row(ln_b[0]), tm=KC * math.gcd(nc, MERGE_CHUNKS))
```

```python
import functools
import math

import jax
import jax.numpy as jnp
import numpy as np
from jax import lax
from jax.experimental import pallas as pl
from jax.experimental.pallas import tpu as pltpu

D_MODEL = 1024
DEPTH = 1
N_META = 16
HEADS = 8
HEAD_DIM = 64
WIDTH = HEADS * HEAD_DIM
TOPK_MAX = 256
REL_BUCKETS = 32
REL_MAX_DIST = 128
LN_EPS = 1e-5
DEEPNORM_ALPHA = (2.0 * DEPTH) ** 0.25
IDX_SCALE = HEADS ** -0.5 * HEAD_DIM ** -0.5
QK_SCALE = HEAD_DIM ** -0.5
LOG2E = math.log2(math.e)
QK_SCALE_LOG2 = QK_SCALE * LOG2E

SUBLANES = 8
LANES = 128
QB = 256
KC = 256
PAIRS = HEADS // 2
PROJ_CHUNKS = 2
MERGE_CHUNKS = 4
MERGE_ROWS = 256
DEN_ROWS = 16
NEG = -1e30
SB_DONE_BITS = 160.0
INT_MIN = -2 ** 31
VMEM_LIMIT = 56 * 1024 * 1024
MXU_DTYPE = jnp.bfloat16
IDX_DTYPE = jnp.float32

_COLS = [WIDTH, WIDTH, WIDTH, WIDTH, WIDTH, HEAD_DIM, HEAD_DIM, WIDTH, WIDTH, HEAD_DIM, HEADS,
         D_MODEL, D_MODEL]
_OFF = [0]
for _c in _COLS:
    _OFF.append(_OFF[-1] + _c)
(O_QA, O_KA, O_VA, O_ZA, O_QB, O_KB, O_VB, O_ZB, O_QI, O_KI, O_WI, O_GA, O_GB, O_END) = _OFF


def _layer_norm(x, g, b):
    mu = jnp.mean(x, axis=-1, keepdims=True)
    xc = x - mu
    var = jnp.mean(xc * xc, axis=-1, keepdims=True)
    return xc * lax.rsqrt(var + LN_EPS) * g + b


def _dot(a, b):
    return jnp.dot(a, b, preferred_element_type=jnp.float32)


def _dot_nt(a, b):
    return lax.dot_general(a, b, (((1,), (1,)), ((), ())), preferred_element_type=jnp.float32)


WQ_COLS = 2 * WIDTH
WQI_COLS = 2 * WIDTH
WK_COLS = WIDTH + LANES
WV_ROWS = WIDTH + HEAD_DIM


def _proj_kernel(x_ref, g_ref, b_ref, perm_ref, wq_ref, wqi_ref, wk_ref, wki_ref, wvt_ref, wit_ref,
                 kng_ref, knb_ref,
                 qa_ref, qb_ref, qi_ref, wi_ref, ka_ref, ki_ref, kb_ref, vat_ref, vbt_ref):
    for c in range(x_ref.shape[1] // KC):
        rows = slice(c * KC, (c + 1) * KC)
        hi = _layer_norm(x_ref[0, rows], g_ref[...], b_ref[...]).astype(IDX_DTYPE)
        h = hi.astype(MXU_DTYPE)
        hp = _dot(perm_ref[...].astype(MXU_DTYPE), h).astype(MXU_DTYPE)
        hpi = _dot(perm_ref[...], hi).astype(IDX_DTYPE)

        def store_pairs(ref, val):
            for pr in range(PAIRS):
                ref[0, pr, rows] = val[:, pr * LANES:(pr + 1) * LANES].astype(ref.dtype)

        store_pairs(qa_ref, _dot(h, wq_ref[:, 0:WIDTH]) * QK_SCALE_LOG2)
        store_pairs(qb_ref, _dot(h, wq_ref[:, WIDTH:WQ_COLS]) * QK_SCALE_LOG2)
        store_pairs(qi_ref, _dot(hi, wqi_ref[:, 0:WIDTH]) * _dot(hi, wqi_ref[:, WIDTH:WQI_COLS]) * IDX_SCALE)
        wi_ref[0, :, rows] = _dot_nt(wit_ref[...], hi)

        store_pairs(ka_ref, _dot(hp, wk_ref[:, 0:WIDTH]))
        kb_ref[0, rows] = _dot(hp, wk_ref[:, WIDTH:WK_COLS]).astype(kb_ref.dtype)
        ki_ref[0, rows] = _layer_norm(_dot(hpi, wki_ref[...]), kng_ref[...], knb_ref[...]).astype(ki_ref.dtype)
        vat = _dot_nt(wvt_ref[0:WIDTH, :], hp)
        vat_ref[0, :, c] = vat.reshape(HEADS, HEAD_DIM, KC).astype(vat_ref.dtype)
        vbt_ref[0, c] = _dot_nt(wvt_ref[WIDTH:WV_ROWS, :], hp).astype(vbt_ref.dtype)


def _proj_call(x, ln_g, ln_b, perm, wq, wqi, wk, wki, wvt, wit, kn_g2, kn_b2, step_chunks):
    bsz, s, d = x.shape
    rows = step_chunks * KC
    assert s % rows == 0 and perm.shape == (KC, KC)
    nc = s // KC
    bf = MXU_DTYPE
    pair_shape = lambda dt: jax.ShapeDtypeStruct((bsz, PAIRS, s, LANES), dt)
    pair_spec = pl.BlockSpec((1, PAIRS, rows, LANES), lambda b, i: (b, 0, i, 0))
    row128 = pl.BlockSpec((1, rows, LANES), lambda b, i: (b, i, 0))
    row_shape = lambda dt: jax.ShapeDtypeStruct((bsz, s, LANES), dt)
    const = lambda shape: pl.BlockSpec(shape, lambda b, i: (0,) * len(shape))
    return pl.pallas_call(
        _proj_kernel,
        grid=(bsz, s // rows),
        in_specs=[pl.BlockSpec((1, rows, d), lambda b, i: (b, i, 0)),
                  const((1, d)), const((1, d)), const((KC, KC)),
                  const((d, WQ_COLS)), const((d, WQI_COLS)), const((d, WK_COLS)), const((d, LANES)),
                  const((WV_ROWS, d)), const((HEADS, d)),
                  const((1, LANES)), const((1, LANES))],
        out_specs=[pair_spec, pair_spec, pair_spec,
                   pl.BlockSpec((1, HEADS, rows), lambda b, i: (b, 0, i)),
                   pair_spec, row128, row128,
                   pl.BlockSpec((1, HEADS, rows // KC, HEAD_DIM, KC), lambda b, i: (b, 0, i, 0, 0)),
                   pl.BlockSpec((1, rows // KC, HEAD_DIM, KC), lambda b, i: (b, i, 0, 0))],
        out_shape=[pair_shape(bf), pair_shape(bf), pair_shape(IDX_DTYPE),
                   jax.ShapeDtypeStruct((bsz, HEADS, s), jnp.float32),
                   pair_shape(bf), row_shape(IDX_DTYPE), row_shape(bf),
                   jax.ShapeDtypeStruct((bsz, HEADS, nc, HEAD_DIM, KC), bf),
                   jax.ShapeDtypeStruct((bsz, nc, HEAD_DIM, KC), bf)],
        compiler_params=pltpu.CompilerParams(
            dimension_semantics=("arbitrary", "arbitrary"), vmem_limit_bytes=VMEM_LIMIT),
        name="proj",
    )(x, ln_g, ln_b, perm, wq, wqi, wk, wki, wvt, wit, kn_g2, kn_b2)


def _group_excl_scan(g_tot, reverse):
    sub = lax.broadcasted_iota(jnp.int32, g_tot.shape, 0)
    out = jnp.zeros_like(g_tot)
    for g in range(SUBLANES):
        row = g_tot[g:g + 1, :]
        take = (sub < g) if reverse else (sub > g)
        out = out + jnp.where(take, row, 0.0)
    return out


def _bit_planes(key):
    assert key.shape[0] == 32 * SUBLANES
    k3 = key.reshape(32, SUBLANES, key.shape[1])
    a = [k3[j] ^ jnp.int32(INT_MIN) for j in range(32)]
    as_i32 = lambda m: jnp.int32(m - (1 << 32) if m >= (1 << 31) else m)
    j, m = 16, 0x0000FFFF
    while j:
        shift = jnp.full(a[0].shape, j, jnp.int32)
        k = 0
        while k < 32:
            t = (a[k] ^ lax.shift_right_logical(a[k + j], shift)) & as_i32(m)
            a[k] = a[k] ^ t
            a[k + j] = a[k + j] ^ lax.shift_left(t, shift)
            k = (k + j + 1) & ~j
        j >>= 1
        m = (m ^ (m << j)) & 0xFFFFFFFF
    return a


def _attn_kernel(qa_ref, qb_ref, qi_ref, wi_ref,
                 ka_ref, vat_ref, ki_ref, kb_ref, vbt_ref,
                 kam_ref, vatm_ref, kim_ref, kbm_ref, vbtm_ref,
                 biasd_ref, biasp_ref, biasm_ref,
                 ya_ref, yb_ref,
                 qpad_scr, qipad_scr, key_scr, mb_scr, sbacc_scr, sbcar_scr, dacc_scr, m_scr, z_scr,
                 plane_scr, alive_scr, ones_scr, *, seq, topk):
    i = pl.program_id(1)
    nslab = KC // SUBLANES
    mslab = N_META // SUBLANES
    f32 = jnp.float32

    lane = lax.broadcasted_iota(jnp.int32, (QB, LANES), 1)
    lo_half = (lane < HEAD_DIM).astype(f32)
    hi_half = (lane >= HEAD_DIM).astype(f32)
    for pr in range(PAIRS):
        for t, ref in enumerate((qa_ref, qb_ref)):
            qp = ref[0, pr]
            qpad_scr[t, 2 * pr] = qp * lo_half.astype(qp.dtype)
            qpad_scr[t, 2 * pr + 1] = qp * hi_half.astype(qp.dtype)
        qp = qi_ref[0, pr]
        qipad_scr[2 * pr] = qp * lo_half.astype(qp.dtype)
        qipad_scr[2 * pr + 1] = qp * hi_half.astype(qp.dtype)

    r_io = lax.broadcasted_iota(jnp.int32, (KC, QB), 0)
    c_io = lax.broadcasted_iota(jnp.int32, (KC, QB), 1)
    pos = (r_io & (SUBLANES - 1)) * nslab + (r_io >> 3)
    mask_lt = pos < c_io
    mask_le = pos <= c_io

    diag0 = pl.multiple_of(i * KC, KC)

    def issue(*streams):
        for logits_of_head, _, rows, branch in streams:
            for h in range(HEADS):
                z_scr[branch, h, 0:rows] = logits_of_head(h)

    def finish(*streams):
        for h in range(HEADS):
            for _, rest_of_head, rows, branch in streams:
                rest_of_head(h, z_scr[branch, h, 0:rows])()

    def sb_chunk(k_of_pair, vt_of_head, mask, n_j, pad_rows, first):
        def logits(h):
            return _dot_nt(k_of_pair(h // 2), qpad_scr[0, h])

        def rest(h, z):
            sp = jnp.maximum(z, 0.0) + jnp.log2(1.0 + jnp.exp2(-jnp.abs(z)))
            if mask is not None:
                sp = jnp.where(mask, sp, 0.0)
            sp3 = sp.reshape(n_j, SUBLANES, QB)
            z3 = z.reshape(n_j, SUBLANES, QB)
            run = jnp.zeros((SUBLANES, QB), f32)
            u = [None] * n_j
            for j in reversed(range(n_j)):
                run = run + sp3[j]
                u[j] = jnp.exp2(z3[j] - run)
            carry = jnp.zeros((SUBLANES, QB), f32) if first else sbcar_scr[h]
            scale = jnp.exp2(-(_group_excl_scan(run, reverse=True) + carry))
            a = jnp.concatenate([u[j] * scale for j in range(n_j)], axis=0)
            if mask is not None:
                a = jnp.where(mask, a, 0.0)
            a = a.astype(MXU_DTYPE)
            if pad_rows:
                a = jnp.concatenate([a, jnp.zeros((pad_rows, QB), MXU_DTYPE)], axis=0)
            sbcar_scr[h] = carry + jnp.sum(run, axis=0, keepdims=True)

            def finish():
                pv = _dot(vt_of_head(h), a)
                sbacc_scr[h] = pv if first else sbacc_scr[h] + pv

            return finish

        return logits, rest, n_j * SUBLANES, 0

    def sb_real_chunk(c, mask, first):
        r0 = pl.multiple_of(c * KC, KC)
        return sb_chunk(lambda pr: ka_ref[0, pr, pl.ds(r0, KC), :], lambda h: vat_ref[0, h, c],
                        mask, nslab, 0, first)

    w = wi_ref[0]
    lo = jnp.where(w >= 0.0, 0.0, -jnp.inf)
    hi = jnp.where(w >= 0.0, jnp.inf, 0.0)

    def score_keys(kic, mask):
        acc = jnp.zeros((kic.shape[0], QB), f32)
        for h in range(HEADS):
            s = _dot_nt(kic, qipad_scr[h])
            acc = acc + jnp.minimum(jnp.maximum(s, lo[h:h + 1, :]), hi[h:h + 1, :])
        if mask is not None:
            acc = jnp.where(mask, acc, -jnp.inf)
        bits = lax.bitcast_convert_type(acc, jnp.int32)
        key = jnp.where(bits < 0, bits ^ jnp.int32(0x7FFFFFFF), bits)
        return jnp.where(bits == jnp.int32(INT_MIN), 0, key)

    ones_word = jnp.full((SUBLANES, QB), -1, jnp.int32)

    def store_chunk_keys(c, r0, mask):
        key = score_keys(ki_ref[0, pl.ds(r0, KC), :], mask)
        key_scr[pl.ds(r0, KC), :] = key
        planes = _bit_planes(key)
        for o in range(32):
            plane_scr[c, o] = planes[o]
        alive_scr[c] = ones_word
        ones_scr[c] = ones_word

    def score_body(c, _):
        store_chunk_keys(c, pl.multiple_of(c * KC, KC), None)
        return 0

    lax.fori_loop(0, i, score_body, 0)
    store_chunk_keys(i, diag0, mask_le)
    meta_keys = score_keys(kim_ref[...], None)
    key_scr[seq:seq + N_META, :] = meta_keys

    def count_meta(cand, strict):
        cb = jnp.broadcast_to(cand, (N_META, QB))
        hit = (meta_keys > cb) if strict else (meta_keys >= cb)
        return jnp.sum(jnp.where(hit, 1.0, 0.0), axis=0, keepdims=True)

    def radix_step(it, st):
        ans, above, took = st
        cand = ans ^ lax.shift_left(jnp.int32(1), 31 - it)
        took_b = jnp.broadcast_to(took, (SUBLANES, QB)) != 0

        def body(c, acc):
            ones = ones_scr[c]
            alive = jnp.where(took_b, ones, alive_scr[c] ^ ones)
            ones = alive & plane_scr[c, it]
            alive_scr[c] = alive
            ones_scr[c] = ones
            return acc + lax.population_count(ones)

        acc = lax.fori_loop(0, i + 1, body, jnp.zeros((SUBLANES, QB), jnp.int32))
        ones_cnt = jnp.sum(acc.astype(f32), axis=0, keepdims=True)
        take = (above + ones_cnt + count_meta(cand, False)) >= float(topk)
        return (jnp.where(take, cand, ans), jnp.where(take, above, above + ones_cnt),
                jnp.where(take, -1, 0).astype(jnp.int32))

    thr, above, _ = lax.fori_loop(
        0, 32, radix_step,
        (jnp.full((1, QB), INT_MIN, jnp.int32), jnp.zeros((1, QB), f32), jnp.full((1, QB), -1, jnp.int32)))

    need = float(topk) - (above + count_meta(thr, True))
    thr_b = jnp.broadcast_to(thr, (SUBLANES, QB))
    need_b = jnp.broadcast_to(need, (SUBLANES, QB))

    def select_rows(blk, n_j, mask, carry):
        b3 = blk.reshape(n_j, SUBLANES, QB)
        tie = [jnp.where(b3[j] == thr_b, 1.0, 0.0) for j in range(n_j)]
        run = jnp.zeros((SUBLANES, QB), f32)
        prefix = []
        for j in range(n_j):
            run = run + tie[j]
            prefix.append(run)
        off = _group_excl_scan(run, reverse=False) + carry
        rows = []
        for j in range(n_j):
            rank = prefix[j] - tie[j] + off
            sel = (b3[j] > thr_b) | ((tie[j] > 0.0) & (rank < need_b))
            rows.append(jnp.where(sel, 0.0, NEG))
        mb = jnp.concatenate(rows, axis=0)
        if mask is not None:
            mb = jnp.where(mask, mb, NEG)
        return mb, carry + jnp.sum(run, axis=0, keepdims=True)

    mb, tie_carry = select_rows(key_scr[seq:seq + N_META, :], mslab, None, jnp.zeros((1, QB), f32))
    mb_scr[seq:seq + N_META, :] = mb

    def select_body(c, carry):
        r0 = pl.multiple_of(c * KC, KC)
        mb_c, carry = select_rows(key_scr[pl.ds(r0, KC), :], nslab, None, carry)
        mb_scr[pl.ds(r0, KC), :] = mb_c
        return carry

    tie_carry = lax.fori_loop(0, i, select_body, tie_carry)
    mb, _ = select_rows(key_scr[pl.ds(diag0, KC), :], nslab, mask_le, tie_carry)
    mb_scr[pl.ds(diag0, KC), :] = mb

    def dsa_chunk(kc_fn, vt_fn, mbias_fn, bias_of_head, rows, pad_rows, first, last):
        def logits(h):
            return _dot_nt(kc_fn(), qpad_scr[1, h])

        def rest(h, qk):
            lg = qk + mbias_fn()
            if bias_of_head is not None:
                lg = lg + bias_of_head(h)
            cmax = jnp.max(lg, axis=0, keepdims=True)
            if first:
                m_new = jnp.broadcast_to(cmax, (SUBLANES, QB))
            else:
                m_old = m_scr[h]
                m_new = jnp.maximum(m_old, cmax)
                alpha = jnp.exp2(m_old - m_new)
            p3 = lg.reshape(lg.shape[0] // SUBLANES, SUBLANES, QB)
            pb = jnp.exp2(p3 - m_new[None]).reshape(lg.shape).astype(MXU_DTYPE)
            if pad_rows:
                pb = jnp.concatenate([pb, jnp.zeros((pad_rows, QB), MXU_DTYPE)], axis=0)
            if not last:
                m_scr[h] = m_new

            def finish():
                vt = vt_fn()
                vt_ones = jnp.concatenate([vt, jnp.ones((DEN_ROWS, vt.shape[1]), vt.dtype)], axis=0)
                pv = _dot(vt_ones, pb)
                acc = pv if first else dacc_scr[h] * alpha[0:1, :] + pv
                if last:
                    yb_ref[0, h] = (acc[0:HEAD_DIM] / acc[HEAD_DIM:HEAD_DIM + 1]).astype(yb_ref.dtype)
                else:
                    dacc_scr[h] = acc

            return finish

        return logits, rest, rows, 1

    def dsa_real_chunk(c, bias_of_head, first):
        r0 = pl.multiple_of(c * KC, KC)
        return dsa_chunk(lambda: kb_ref[0, pl.ds(r0, KC), :], lambda: vbt_ref[0, c],
                         lambda: mb_scr[pl.ds(r0, KC), :], bias_of_head, KC, 0, first, False)

    def run_pair(*streams):
        issue(*streams)
        finish(*streams)

    def pair(p, bias_of_head=None):
        return sb_real_chunk(i - 1 - p, None, False), dsa_real_chunk(p, bias_of_head, False)

    def sb_open():
        return jnp.min(jnp.min(sbcar_scr[...], axis=0)) < SB_DONE_BITS

    run_pair(sb_real_chunk(i, mask_lt, True), dsa_real_chunk(i, lambda h: biasd_ref[h], True))

    def both_cond(st):
        p, still_open = st
        return (p < i - 1) & still_open

    def both_body(st):
        run_pair(*pair(st[0]))
        return st[0] + 1, sb_open()

    p_sparse, _ = lax.while_loop(both_cond, both_body, (jnp.int32(0), sb_open()))

    def sparse_only(p, _):
        run_pair(dsa_real_chunk(p, None, False))
        return 0

    lax.fori_loop(p_sparse, i - 1, sparse_only, 0)

    prev_bias = lambda h: biasp_ref[h]
    open_prev = sb_open()

    @pl.when((i > 0) & open_prev)
    def _():
        run_pair(*pair(i - 1, prev_bias))

    @pl.when((i > 0) & jnp.logical_not(open_prev))
    def _():
        run_pair(dsa_real_chunk(i - 1, prev_bias, False))

    first_block = (i == 0).astype(f32)
    sb_meta = sb_chunk(lambda pr: kam_ref[pr], lambda h: vatm_ref[h], None, mslab, LANES - N_META, False)
    dsa_meta = dsa_chunk(lambda: kbm_ref[...], lambda: vbtm_ref[...], lambda: mb_scr[seq:seq + N_META, :],
                         lambda h: biasm_ref[h] * first_block, N_META, LANES - N_META, False, True)
    open_meta = sb_open()

    @pl.when(open_meta)
    def _():
        run_pair(sb_meta, dsa_meta)

    @pl.when(jnp.logical_not(open_meta))
    def _():
        run_pair(dsa_meta)

    for h in range(HEADS):
        ya_ref[0, h] = sbacc_scr[h].astype(ya_ref.dtype)


def _attn_call(qa, qb, qi, wit, ka, vat, ki, kb, vbt, kam, vatm, kim, kbm, vbtm,
               bias_d, bias_p, bias_m, topk):
    bsz, _, s, _ = qa.shape
    nq = s // QB
    nc = s // KC
    bf = MXU_DTYPE
    qspec = pl.BlockSpec((1, PAIRS, QB, LANES), lambda b, i: (b, 0, i, 0))
    const = lambda shape: pl.BlockSpec(shape, lambda b, i: (0,) * len(shape))
    yspec = pl.BlockSpec((1, HEADS, HEAD_DIM, QB), lambda b, i: (b, 0, 0, i))
    yshape = jax.ShapeDtypeStruct((bsz, HEADS, HEAD_DIM, s), bf)
    return pl.pallas_call(
        functools.partial(_attn_kernel, seq=s, topk=topk),
        grid=(bsz, nq),
        in_specs=[qspec, qspec, qspec,
                  pl.BlockSpec((1, HEADS, QB), lambda b, i: (b, 0, i)),
                  pl.BlockSpec((1, PAIRS, s, LANES), lambda b, i: (b, 0, 0, 0)),
                  pl.BlockSpec((1, HEADS, nc, HEAD_DIM, KC), lambda b, i: (b, 0, 0, 0, 0)),
                  pl.BlockSpec((1, s, LANES), lambda b, i: (b, 0, 0)),
                  pl.BlockSpec((1, s, LANES), lambda b, i: (b, 0, 0)),
                  pl.BlockSpec((1, nc, HEAD_DIM, KC), lambda b, i: (b, 0, 0, 0)),
                  const((PAIRS, N_META, LANES)), const((HEADS, HEAD_DIM, LANES)),
                  const((N_META, LANES)), const((N_META, LANES)), const((HEAD_DIM, LANES)),
                  const((HEADS, KC, QB)), const((HEADS, KC, QB)), const((HEADS, N_META, QB))],
        out_specs=[yspec, yspec],
        out_shape=[yshape, yshape],
        scratch_shapes=[pltpu.VMEM((2, HEADS, QB, LANES), bf),
                        pltpu.VMEM((HEADS, QB, LANES), IDX_DTYPE),
                        pltpu.VMEM((s + N_META, QB), jnp.int32),
                        pltpu.VMEM((s + N_META, QB), jnp.float32),
                        pltpu.VMEM((HEADS, HEAD_DIM, QB), jnp.float32),
                        pltpu.VMEM((HEADS, SUBLANES, QB), jnp.float32),
                        pltpu.VMEM((HEADS, HEAD_DIM + DEN_ROWS, QB), jnp.float32),
                        pltpu.VMEM((HEADS, SUBLANES, QB), jnp.float32),
                        pltpu.VMEM((2, HEADS, KC, QB), jnp.float32),
                        pltpu.VMEM((nc, 32, SUBLANES, QB), jnp.int32),
                        pltpu.VMEM((nc, SUBLANES, QB), jnp.int32),
                        pltpu.VMEM((nc, SUBLANES, QB), jnp.int32)],
        compiler_params=pltpu.CompilerParams(
            dimension_semantics=("arbitrary", "arbitrary"), vmem_limit_bytes=VMEM_LIMIT),
        name="attn",
    )(qa, qb, qi, wit, ka, vat, ki, kb, vbt, kam, vatm, kim, kbm, vbtm, bias_d, bias_p, bias_m)


def _sigmoid(x):
    return 1.0 / (1.0 + jnp.exp(-x))


def _merge_kernel(x_ref, ya_ref, yb_ref, g_in_ref, b_in_ref, wz_ref, bg_ref, wpa_ref, wpb_ref,
                  wo_ref, g_ref, b_ref, o_ref):
    for t in range(x_ref.shape[1] // MERGE_ROWS):
        rows = slice(t * MERGE_ROWS, (t + 1) * MERGE_ROWS)
        h = _layer_norm(x_ref[0, rows], g_in_ref[...], b_in_ref[...])
        hb = h.astype(MXU_DTYPE)

        def branch(yt_ref, z_col, g_col, bg_col, wp_ref):
            z = _dot(hb, wz_ref[:, z_col:z_col + WIDTH])
            y = yt_ref[0, :, rows].astype(jnp.float32).T
            y = (y * (z * _sigmoid(z))).astype(MXU_DTYPE)
            gate = _sigmoid(_dot(hb, wz_ref[:, g_col:g_col + D_MODEL]) + bg_ref[:, bg_col:bg_col + D_MODEL])
            return gate * _dot(y, wp_ref[...])

        merged = (branch(ya_ref, 0, 2 * WIDTH, 0, wpa_ref)
                  + branch(yb_ref, WIDTH, 2 * WIDTH + D_MODEL, D_MODEL, wpb_ref))
        out = _dot(merged.astype(MXU_DTYPE), wo_ref[...])
        o_ref[0, rows] = _layer_norm(DEEPNORM_ALPHA * h + out, g_ref[...], b_ref[...])


def _merge_call(x, ya, yb, ln_in_g, ln_in_b, wz, bg, wpa, wpb, wo, ln_g, ln_b, tm):
    bsz, s, d = x.shape
    assert s % tm == 0
    const = lambda shape: pl.BlockSpec(shape, lambda b, i: (0,) * len(shape))
    row = lambda width: pl.BlockSpec((1, tm, width), lambda b, i: (b, i, 0))
    col = pl.BlockSpec((1, WIDTH, tm), lambda b, i: (b, 0, i))
    return pl.pallas_call(
        _merge_kernel,
        grid=(bsz, s // tm),
        in_specs=[row(d), col, col, const((1, d)), const((1, d)),
                  const((d, 2 * WIDTH + 2 * d)), const((1, 2 * d)),
                  const((WIDTH, d)), const((WIDTH, d)), const((d, d)), const((1, d)), const((1, d))],
        out_specs=row(d),
        out_shape=jax.ShapeDtypeStruct((bsz, s, d), jnp.float32),
        compiler_params=pltpu.CompilerParams(
            dimension_semantics=("arbitrary", "arbitrary"), vmem_limit_bytes=VMEM_LIMIT),
        name="merge",
    )(x, ya, yb, ln_in_g, ln_in_b, wz, bg, wpa, wpb, wo, ln_g, ln_b)


def _perm_rows(a, axis, chunk):
    shp = a.shape
    n = shp[axis] // chunk
    a = a.reshape(shp[:axis] + (n, SUBLANES, chunk // SUBLANES) + shp[axis + 1:])
    a = jnp.swapaxes(a, axis + 1, axis + 2)
    return a.reshape(shp)


def _rel_bucket(dist):
    max_exact = REL_BUCKETS // 2
    nf = np.maximum(dist, 1).astype(np.float32)
    large = max_exact + (np.log(nf / np.float32(max_exact)) / np.float32(math.log(REL_MAX_DIST / max_exact))
                         * np.float32(REL_BUCKETS - max_exact)).astype(np.int32)
    large = np.minimum(large, REL_BUCKETS - 1)
    return np.where(dist < max_exact, dist, large)


def _bias_tiles(rel_bias):
    r = np.arange(KC)
    pos = (r % SUBLANES) * (KC // SUBLANES) + r // SUBLANES
    q = np.arange(QB)
    rm = np.arange(N_META)
    posm = (rm % SUBLANES) * (N_META // SUBLANES) + rm // SUBLANES
    table = (rel_bias - rel_bias[REL_BUCKETS - 1]).astype(jnp.float32) * LOG2E

    def tile(dist):
        onehot = jnp.asarray(_rel_bucket(dist)[..., None] == np.arange(REL_BUCKETS), jnp.float32)
        return jnp.einsum("rqb,bh->hrq", onehot, table, precision=lax.Precision.HIGHEST)

    return (tile(np.maximum(q[None, :] - pos[:, None], 0)),
            tile(KC + q[None, :] - pos[:, None]),
            tile(N_META + q[None, :] - posm[:, None]))


def kernel(x, meta_tokens, ln_in_g, ln_in_b, rel_bias, w_in, b_gate, idx_kn_g, idx_kn_b,
           w_pa, w_pb, w_o, ln_g, ln_b):
    bsz, seq, d = x.shape
    assert d == D_MODEL and seq % KC == 0 and w_in.shape[0] == DEPTH == 1
    topk = min(TOPK_MAX, seq // 4)
    nc = seq // KC
    bf = MXU_DTYPE
    w = w_in[0]
    dup = lambda a: jnp.concatenate([a, a], axis=-1)
    cols = lambda off, width: w[:, off:off + width]
    w_q = jnp.concatenate([cols(O_QA, WIDTH), cols(O_QB, WIDTH)], axis=1).astype(bf)
    w_qi = jnp.concatenate([cols(O_QI, WIDTH), jnp.repeat(cols(O_WI, HEADS), HEAD_DIM, axis=1)],
                           axis=1).astype(IDX_DTYPE)
    w_k = jnp.concatenate([cols(O_KA, WIDTH), dup(cols(O_KB, HEAD_DIM))], axis=1).astype(bf)
    w_ki = dup(cols(O_KI, HEAD_DIM)).astype(IDX_DTYPE)
    w_vt = jnp.concatenate([cols(O_VA, WIDTH), cols(O_VB, HEAD_DIM)], axis=1).T.astype(bf)
    w_wit = cols(O_WI, HEADS).T.astype(IDX_DTYPE)
    w_z = jnp.concatenate([cols(O_ZA, WIDTH), cols(O_ZB, WIDTH), cols(O_GA, D_MODEL), cols(O_GB, D_MODEL)],
                          axis=1).astype(bf)
    row = lambda a: a.reshape(1, -1).astype(jnp.float32)
    r = np.arange(KC)
    perm = np.zeros((KC, KC), np.float32)
    perm[r, (r % SUBLANES) * (KC // SUBLANES) + r // SUBLANES] = 1.0

    proj = functools.partial(_proj_call, ln_g=row(ln_in_g), ln_b=row(ln_in_b), wq=w_q, wqi=w_qi, wk=w_k,
                             wki=w_ki, wvt=w_vt, wit=w_wit,
                             kn_g2=row(dup(idx_kn_g[0])), kn_b2=row(dup(idx_kn_b[0])))
    qa, qb, qi, wit, ka, ki, kb, vat, vbt = proj(x, perm=jnp.asarray(perm, IDX_DTYPE),
                                                 step_chunks=math.gcd(nc, PROJ_CHUNKS))
    x_meta = jnp.pad(meta_tokens.astype(x.dtype), ((0, KC - N_META), (0, 0)))[None]
    _, _, _, _, kam, kim, kbm, vatm, vbtm = proj(x_meta, perm=jnp.eye(KC, dtype=IDX_DTYPE), step_chunks=1)
    pad_keys = lambda a: jnp.pad(a, [(0, 0)] * (a.ndim - 1) + [(0, LANES - N_META)])
    kam = _perm_rows(kam[0, :, :N_META], 1, N_META)
    kim = _perm_rows(kim[0, :N_META], 0, N_META)
    kbm = _perm_rows(kbm[0, :N_META], 0, N_META)
    vatm = pad_keys(_perm_rows(vatm[0, :, 0, :, :N_META], 2, N_META))
    vbtm = pad_keys(_perm_rows(vbtm[0, 0, :, :N_META], 1, N_META))
    bias_d, bias_p, bias_m = _bias_tiles(rel_bias)

    yat, ybt = _attn_call(qa, qb, qi, wit, ka, vat, ki, kb, vbt, kam, vatm, kim, kbm, vbtm,
                          bias_d, bias_p, bias_m, topk)
    ya = yat.reshape(bsz, WIDTH, seq)
    yb = ybt.reshape(bsz, WIDTH, seq)

    return _merge_call(x, ya, yb, row(ln_in_g), row(ln_in_b), w_z, row(b_gate[0]),
                       w_pa[0].astype(bf), w_pb[0].astype(bf), w_o[0].astype(bf),
                       row(ln_g[0]), row(ln_b[0]), tm=KC * math.gcd(nc, MERGE_CHUNKS))
```

```python
import functools
import math

import jax
import jax.numpy as jnp
import numpy as np
from jax import lax
from jax.experimental import pallas as pl
from jax.experimental.pallas import tpu as pltpu

D_MODEL = 1024
DEPTH = 1
N_META = 16
HEADS = 8
HEAD_DIM = 64
WIDTH = HEADS * HEAD_DIM
TOPK_MAX = 256
REL_BUCKETS = 32
REL_MAX_DIST = 128
LN_EPS = 1e-5
DEEPNORM_ALPHA = (2.0 * DEPTH) ** 0.25
IDX_SCALE = HEADS ** -0.5 * HEAD_DIM ** -0.5
QK_SCALE = HEAD_DIM ** -0.5
LOG2E = math.log2(math.e)
QK_SCALE_LOG2 = QK_SCALE * LOG2E

SUBLANES = 8
LANES = 128
QB = 256
KC = 256
PAIRS = HEADS // 2
PROJ_CHUNKS = 2
MERGE_CHUNKS = 4
MERGE_ROWS = 256
DEN_ROWS = 16
NEG = -1e30
SB_DONE_BITS = 160.0
INT_MIN = -2 ** 31
VMEM_LIMIT = 56 * 1024 * 1024
MXU_DTYPE = jnp.bfloat16
IDX_DTYPE = jnp.float32

_COLS = [WIDTH, WIDTH, WIDTH, WIDTH, WIDTH, HEAD_DIM, HEAD_DIM, WIDTH, WIDTH, HEAD_DIM, HEADS,
         D_MODEL, D_MODEL]
_OFF = [0]
for _c in _COLS:
    _OFF.append(_OFF[-1] + _c)
(O_QA, O_KA, O_VA, O_ZA, O_QB, O_KB, O_VB, O_ZB, O_QI, O_KI, O_WI, O_GA, O_GB, O_END) = _OFF


def _layer_norm(x, g, b):
    mu = jnp.mean(x, axis=-1, keepdims=True)
    xc = x - mu
    var = jnp.mean(xc * xc, axis=-1, keepdims=True)
    return xc * lax.rsqrt(var + LN_EPS) * g + b


def _dot(a, b):
    return jnp.dot(a, b, preferred_element_type=jnp.float32)


def _dot_nt(a, b):
    return lax.dot_general(a, b, (((1,), (1,)), ((), ())), preferred_element_type=jnp.float32)


WQ_COLS = 2 * WIDTH
WQI_COLS = 2 * WIDTH
WK_COLS = WIDTH + LANES
WV_ROWS = WIDTH + HEAD_DIM


def _proj_kernel(x_ref, g_ref, b_ref, perm_ref, wq_ref, wqi_ref, wk_ref, wki_ref, wvt_ref, wit_ref,
                 kng_ref, knb_ref,
                 qa_ref, qb_ref, qi_ref, wi_ref, ka_ref, ki_ref, kb_ref, vat_ref, vbt_ref):
    for c in range(x_ref.shape[1] // KC):
        rows = slice(c * KC, (c + 1) * KC)
        hi = _layer_norm(x_ref[0, rows], g_ref[...], b_ref[...]).astype(IDX_DTYPE)
        h = hi.astype(MXU_DTYPE)
        hp = _dot(perm_ref[...].astype(MXU_DTYPE), h).astype(MXU_DTYPE)
        hpi = _dot(perm_ref[...], hi).astype(IDX_DTYPE)

        def store_pairs(ref, val):
            for pr in range(PAIRS):
                ref[0, pr, rows] = val[:, pr * LANES:(pr + 1) * LANES].astype(ref.dtype)

        store_pairs(qa_ref, _dot(h, wq_ref[:, 0:WIDTH]) * QK_SCALE_LOG2)
        store_pairs(qb_ref, _dot(h, wq_ref[:, WIDTH:WQ_COLS]) * QK_SCALE_LOG2)
        store_pairs(qi_ref, _dot(hi, wqi_ref[:, 0:WIDTH]) * _dot(hi, wqi_ref[:, WIDTH:WQI_COLS]) * IDX_SCALE)
        wi_ref[0, :, rows] = _dot_nt(wit_ref[...], hi)

        store_pairs(ka_ref, _dot(hp, wk_ref[:, 0:WIDTH]))
        kb_ref[0, rows] = _dot(hp, wk_ref[:, WIDTH:WK_COLS]).astype(kb_ref.dtype)
        ki_ref[0, rows] = _layer_norm(_dot(hpi, wki_ref[...]), kng_ref[...], knb_ref[...]).astype(ki_ref.dtype)
        vat = _dot_nt(wvt_ref[0:WIDTH, :], hp)
        vat_ref[0, :, c] = vat.reshape(HEADS, HEAD_DIM, KC).astype(vat_ref.dtype)
        vbt_ref[0, c] = _dot_nt(wvt_ref[WIDTH:WV_ROWS, :], hp).astype(vbt_ref.dtype)


def _proj_call(x, ln_g, ln_b, perm, wq, wqi, wk, wki, wvt, wit, kn_g2, kn_b2, step_chunks):
    bsz, s, d = x.shape
    rows = step_chunks * KC
    assert s % rows == 0 and perm.shape == (KC, KC)
    nc = s // KC
    bf = MXU_DTYPE
    pair_shape = lambda dt: jax.ShapeDtypeStruct((bsz, PAIRS, s, LANES), dt)
    pair_spec = pl.BlockSpec((1, PAIRS, rows, LANES), lambda b, i: (b, 0, i, 0))
    row128 = pl.BlockSpec((1, rows, LANES), lambda b, i: (b, i, 0))
    row_shape = lambda dt: jax.ShapeDtypeStruct((bsz, s, LANES), dt)
    const = lambda shape: pl.BlockSpec(shape, lambda b, i: (0,) * len(shape))
    return pl.pallas_call(
        _proj_kernel,
        grid=(bsz, s // rows),
        in_specs=[pl.BlockSpec((1, rows, d), lambda b, i: (b, i, 0)),
                  const((1, d)), const((1, d)), const((KC, KC)),
                  const((d, WQ_COLS)), const((d, WQI_COLS)), const((d, WK_COLS)), const((d, LANES)),
                  const((WV_ROWS, d)), const((HEADS, d)),
                  const((1, LANES)), const((1, LANES))],
        out_specs=[pair_spec, pair_spec, pair_spec,
                   pl.BlockSpec((1, HEADS, rows), lambda b, i: (b, 0, i)),
                   pair_spec, row128, row128,
                   pl.BlockSpec((1, HEADS, rows // KC, HEAD_DIM, KC), lambda b, i: (b, 0, i, 0, 0)),
                   pl.BlockSpec((1, rows // KC, HEAD_DIM, KC), lambda b, i: (b, i, 0, 0))],
        out_shape=[pair_shape(bf), pair_shape(bf), pair_shape(IDX_DTYPE),
                   jax.ShapeDtypeStruct((bsz, HEADS, s), jnp.float32),
                   pair_shape(bf), row_shape(IDX_DTYPE), row_shape(bf),
                   jax.ShapeDtypeStruct((bsz, HEADS, nc, HEAD_DIM, KC), bf),
                   jax.ShapeDtypeStruct((bsz, nc, HEAD_DIM, KC), bf)],
        compiler_params=pltpu.CompilerParams(
            dimension_semantics=("arbitrary", "arbitrary"), vmem_limit_bytes=VMEM_LIMIT),
        name="proj",
    )(x, ln_g, ln_b, perm, wq, wqi, wk, wki, wvt, wit, kn_g2, kn_b2)


def _group_excl_scan(g_tot, reverse):
    sub = lax.broadcasted_iota(jnp.int32, g_tot.shape, 0)
    out = jnp.zeros_like(g_tot)
    for g in range(SUBLANES):
        row = g_tot[g:g + 1, :]
        take = (sub < g) if reverse else (sub > g)
        out = out + jnp.where(take, row, 0.0)
    return out


def _bit_planes(key):
    assert key.shape[0] == 32 * SUBLANES
    k3 = key.reshape(32, SUBLANES, key.shape[1])
    a = [k3[j] ^ jnp.int32(INT_MIN) for j in range(32)]
    as_i32 = lambda m: jnp.int32(m - (1 << 32) if m >= (1 << 31) else m)
    j, m = 16, 0x0000FFFF
    while j:
        shift = jnp.full(a[0].shape, j, jnp.int32)
        k = 0
        while k < 32:
            t = (a[k] ^ lax.shift_right_logical(a[k + j], shift)) & as_i32(m)
            a[k] = a[k] ^ t
            a[k + j] = a[k + j] ^ lax.shift_left(t, shift)
            k = (k + j + 1) & ~j
        j >>= 1
        m = (m ^ (m << j)) & 0xFFFFFFFF
    return a


def _attn_kernel(qa_ref, qb_ref, qi_ref, wi_ref,
                 ka_ref, vat_ref, ki_ref, kb_ref, vbt_ref,
                 kam_ref, vatm_ref, kim_ref, kbm_ref, vbtm_ref,
                 biasd_ref, biasp_ref, biasm_ref,
                 ya_ref, yb_ref,
                 qpad_scr, qipad_scr, key_scr, mb_scr, sbacc_scr, sbcar_scr, dacc_scr, m_scr, z_scr,
                 plane_scr, alive_scr, ones_scr, *, seq, topk):
    i = pl.program_id(1)
    nslab = KC // SUBLANES
    mslab = N_META // SUBLANES
    f32 = jnp.float32

    lane = lax.broadcasted_iota(jnp.int32, (QB, LANES), 1)
    lo_half = (lane < HEAD_DIM).astype(f32)
    hi_half = (lane >= HEAD_DIM).astype(f32)
    for pr in range(PAIRS):
        for t, ref in enumerate((qa_ref, qb_ref)):
            qp = ref[0, pr]
            qpad_scr[t, 2 * pr] = qp * lo_half.astype(qp.dtype)
            qpad_scr[t, 2 * pr + 1] = qp * hi_half.astype(qp.dtype)
        qp = qi_ref[0, pr]
        qipad_scr[2 * pr] = qp * lo_half.astype(qp.dtype)
        qipad_scr[2 * pr + 1] = qp * hi_half.astype(qp.dtype)

    r_io = lax.broadcasted_iota(jnp.int32, (KC, QB), 0)
    c_io = lax.broadcasted_iota(jnp.int32, (KC, QB), 1)
    pos = (r_io & (SUBLANES - 1)) * nslab + (r_io >> 3)
    mask_lt = pos < c_io
    mask_le = pos <= c_io

    diag0 = pl.multiple_of(i * KC, KC)

    def issue(*streams):
        for logits_of_head, _, rows, branch in streams:
            for h in range(HEADS):
                z_scr[branch, h, 0:rows] = logits_of_head(h)

    def finish(*streams):
        for h in range(HEADS):
            for _, rest_of_head, rows, branch in streams:
                rest_of_head(h, z_scr[branch, h, 0:rows])()

    def sb_chunk(k_of_pair, vt_of_head, mask, n_j, pad_rows, first):
        def logits(h):
            return _dot_nt(k_of_pair(h // 2), qpad_scr[0, h])

        def rest(h, z):
            sp = jnp.maximum(z, 0.0) + jnp.log2(1.0 + jnp.exp2(-jnp.abs(z)))
            if mask is not None:
                sp = jnp.where(mask, sp, 0.0)
            sp3 = sp.reshape(n_j, SUBLANES, QB)
            z3 = z.reshape(n_j, SUBLANES, QB)
            run = jnp.zeros((SUBLANES, QB), f32)
            u = [None] * n_j
            for j in reversed(range(n_j)):
                run = run + sp3[j]
                u[j] = jnp.exp2(z3[j] - run)
            carry = jnp.zeros((SUBLANES, QB), f32) if first else sbcar_scr[h]
            scale = jnp.exp2(-(_group_excl_scan(run, reverse=True) + carry))
            a = jnp.concatenate([u[j] * scale for j in range(n_j)], axis=0)
            if mask is not None:
                a = jnp.where(mask, a, 0.0)
            a = a.astype(MXU_DTYPE)
            if pad_rows:
                a = jnp.concatenate([a, jnp.zeros((pad_rows, QB), MXU_DTYPE)], axis=0)
            sbcar_scr[h] = carry + jnp.sum(run, axis=0, keepdims=True)

            def finish():
                pv = _dot(vt_of_head(h), a)
                sbacc_scr[h] = pv if first else sbacc_scr[h] + pv

            return finish

        return logits, rest, n_j * SUBLANES, 0

    def sb_real_chunk(c, mask, first):
        r0 = pl.multiple_of(c * KC, KC)
        return sb_chunk(lambda pr: ka_ref[0, pr, pl.ds(r0, KC), :], lambda h: vat_ref[0, h, c],
                        mask, nslab, 0, first)

    w = wi_ref[0]
    lo = jnp.where(w >= 0.0, 0.0, -jnp.inf)
    hi = jnp.where(w >= 0.0, jnp.inf, 0.0)

    def score_keys(kic, mask):
        acc = jnp.zeros((kic.shape[0], QB), f32)
        for h in range(HEADS):
            s = _dot_nt(kic, qipad_scr[h])
            acc = acc + jnp.minimum(jnp.maximum(s, lo[h:h + 1, :]), hi[h:h + 1, :])
        if mask is not None:
            acc = jnp.where(mask, acc, -jnp.inf)
        bits = lax.bitcast_convert_type(acc, jnp.int32)
        key = jnp.where(bits < 0, bits ^ jnp.int32(0x7FFFFFFF), bits)
        return jnp.where(bits == jnp.int32(INT_MIN), 0, key)

    nchunks = seq // KC

    @pl.when(i == 0)
    def _():
        for c in range(1, nchunks):
            plane_scr[c] = jnp.zeros((32, SUBLANES, QB), jnp.int32)

    def store_chunk_keys(c, r0, mask):
        key = score_keys(ki_ref[0, pl.ds(r0, KC), :], mask)
        key_scr[pl.ds(r0, KC), :] = key
        planes = _bit_planes(key)
        for o in range(32):
            plane_scr[c, o] = planes[o]

    def score_body(c, _):
        store_chunk_keys(c, pl.multiple_of(c * KC, KC), None)
        return 0

    lax.fori_loop(0, i, score_body, 0)
    store_chunk_keys(i, diag0, mask_le)
    meta_keys = score_keys(kim_ref[...], None)
    key_scr[seq:seq + N_META, :] = meta_keys

    def count_meta(cand, strict):
        cb = jnp.broadcast_to(cand, (N_META, QB))
        hit = (meta_keys > cb) if strict else (meta_keys >= cb)
        return jnp.sum(jnp.where(hit, 1.0, 0.0), axis=0, keepdims=True)

    def radix_step(it, st):
        ans, above, took = st
        cand = ans ^ lax.shift_left(jnp.int32(1), 31 - it)
        took_b = jnp.broadcast_to(took, (SUBLANES, QB)) != 0
        acc = jnp.zeros((SUBLANES, QB), jnp.int32)
        for c in range(nchunks):
            ones = ones_scr[c]
            alive = jnp.where(took_b, ones, alive_scr[c] ^ ones)
            ones = alive & plane_scr[c, it]
            alive_scr[c] = alive
            ones_scr[c] = ones
            acc = acc + lax.population_count(ones)
        ones_cnt = jnp.sum(acc.astype(f32), axis=0, keepdims=True)
        take = (above + ones_cnt + count_meta(cand, False)) >= float(topk)
        return (jnp.where(take, cand, ans), jnp.where(take, above, above + ones_cnt),
                jnp.where(take, -1, 0).astype(jnp.int32))

    for c in range(nchunks):
        alive_scr[c] = jnp.full((SUBLANES, QB), -1, jnp.int32)
        ones_scr[c] = jnp.full((SUBLANES, QB), -1, jnp.int32)

    thr, above, _ = lax.fori_loop(
        0, 32, radix_step,
        (jnp.full((1, QB), INT_MIN, jnp.int32), jnp.zeros((1, QB), f32), jnp.full((1, QB), -1, jnp.int32)))

    need = float(topk) - (above + count_meta(thr, True))
    thr_b = jnp.broadcast_to(thr, (SUBLANES, QB))
    need_b = jnp.broadcast_to(need, (SUBLANES, QB))

    def select_rows(blk, n_j, mask, carry):
        b3 = blk.reshape(n_j, SUBLANES, QB)
        tie = [jnp.where(b3[j] == thr_b, 1.0, 0.0) for j in range(n_j)]
        run = jnp.zeros((SUBLANES, QB), f32)
        prefix = []
        for j in range(n_j):
            run = run + tie[j]
            prefix.append(run)
        off = _group_excl_scan(run, reverse=False) + carry
        rows = []
        for j in range(n_j):
            rank = prefix[j] - tie[j] + off
            sel = (b3[j] > thr_b) | ((tie[j] > 0.0) & (rank < need_b))
            rows.append(jnp.where(sel, 0.0, NEG))
        mb = jnp.concatenate(rows, axis=0)
        if mask is not None:
            mb = jnp.where(mask, mb, NEG)
        return mb, carry + jnp.sum(run, axis=0, keepdims=True)

    mb, tie_carry = select_rows(key_scr[seq:seq + N_META, :], mslab, None, jnp.zeros((1, QB), f32))
    mb_scr[seq:seq + N_META, :] = mb

    def select_body(c, carry):
        r0 = pl.multiple_of(c * KC, KC)
        mb_c, carry = select_rows(key_scr[pl.ds(r0, KC), :], nslab, None, carry)
        mb_scr[pl.ds(r0, KC), :] = mb_c
        return carry

    tie_carry = lax.fori_loop(0, i, select_body, tie_carry)
    mb, _ = select_rows(key_scr[pl.ds(diag0, KC), :], nslab, mask_le, tie_carry)
    mb_scr[pl.ds(diag0, KC), :] = mb

    def dsa_chunk(kc_fn, vt_fn, mbias_fn, bias_of_head, rows, pad_rows, first, last):
        def logits(h):
            return _dot_nt(kc_fn(), qpad_scr[1, h])

        def rest(h, qk):
            lg = qk + mbias_fn()
            if bias_of_head is not None:
                lg = lg + bias_of_head(h)
            cmax = jnp.max(lg, axis=0, keepdims=True)
            if first:
                m_new = jnp.broadcast_to(cmax, (SUBLANES, QB))
            else:
                m_old = m_scr[h]
                m_new = jnp.maximum(m_old, cmax)
                alpha = jnp.exp2(m_old - m_new)
            p3 = lg.reshape(lg.shape[0] // SUBLANES, SUBLANES, QB)
            pb = jnp.exp2(p3 - m_new[None]).reshape(lg.shape).astype(MXU_DTYPE)
            if pad_rows:
                pb = jnp.concatenate([pb, jnp.zeros((pad_rows, QB), MXU_DTYPE)], axis=0)
            if not last:
                m_scr[h] = m_new

            def finish():
                vt = vt_fn()
                vt_ones = jnp.concatenate([vt, jnp.ones((DEN_ROWS, vt.shape[1]), vt.dtype)], axis=0)
                pv = _dot(vt_ones, pb)
                acc = pv if first else dacc_scr[h] * alpha[0:1, :] + pv
                if last:
                    yb_ref[0, h] = (acc[0:HEAD_DIM] / acc[HEAD_DIM:HEAD_DIM + 1]).astype(yb_ref.dtype)
                else:
                    dacc_scr[h] = acc

            return finish

        return logits, rest, rows, 1

    def dsa_real_chunk(c, bias_of_head, first):
        r0 = pl.multiple_of(c * KC, KC)
        return dsa_chunk(lambda: kb_ref[0, pl.ds(r0, KC), :], lambda: vbt_ref[0, c],
                         lambda: mb_scr[pl.ds(r0, KC), :], bias_of_head, KC, 0, first, False)

    def run_pair(*streams):
        issue(*streams)
        finish(*streams)

    def pair(p, bias_of_head=None):
        return sb_real_chunk(i - 1 - p, None, False), dsa_real_chunk(p, bias_of_head, False)

    def sb_open():
        return jnp.min(jnp.min(sbcar_scr[...], axis=0)) < SB_DONE_BITS

    run_pair(sb_real_chunk(i, mask_lt, True), dsa_real_chunk(i, lambda h: biasd_ref[h], True))

    def both_cond(st):
        p, still_open = st
        return (p < i - 1) & still_open

    def both_body(st):
        run_pair(*pair(st[0]))
        return st[0] + 1, sb_open()

    p_sparse, _ = lax.while_loop(both_cond, both_body, (jnp.int32(0), sb_open()))

    def sparse_only(p, _):
        run_pair(dsa_real_chunk(p, None, False))
        return 0

    lax.fori_loop(p_sparse, i - 1, sparse_only, 0)

    prev_bias = lambda h: biasp_ref[h]
    open_prev = sb_open()

    @pl.when((i > 0) & open_prev)
    def _():
        run_pair(*pair(i - 1, prev_bias))

    @pl.when((i > 0) & jnp.logical_not(open_prev))
    def _():
        run_pair(dsa_real_chunk(i - 1, prev_bias, False))

    first_block = (i == 0).astype(f32)
    sb_meta = sb_chunk(lambda pr: kam_ref[pr], lambda h: vatm_ref[h], None, mslab, LANES - N_META, False)
    dsa_meta = dsa_chunk(lambda: kbm_ref[...], lambda: vbtm_ref[...], lambda: mb_scr[seq:seq + N_META, :],
                         lambda h: biasm_ref[h] * first_block, N_META, LANES - N_META, False, True)
    open_meta = sb_open()

    @pl.when(open_meta)
    def _():
        run_pair(sb_meta, dsa_meta)

    @pl.when(jnp.logical_not(open_meta))
    def _():
        run_pair(dsa_meta)

    for h in range(HEADS):
        ya_ref[0, h] = sbacc_scr[h].astype(ya_ref.dtype)


def _attn_call(qa, qb, qi, wit, ka, vat, ki, kb, vbt, kam, vatm, kim, kbm, vbtm,
               bias_d, bias_p, bias_m, topk):
    bsz, _, s, _ = qa.shape
    nq = s // QB
    nc = s // KC
    bf = MXU_DTYPE
    qspec = pl.BlockSpec((1, PAIRS, QB, LANES), lambda b, i: (b, 0, i, 0))
    const = lambda shape: pl.BlockSpec(shape, lambda b, i: (0,) * len(shape))
    yspec = pl.BlockSpec((1, HEADS, HEAD_DIM, QB), lambda b, i: (b, 0, 0, i))
    yshape = jax.ShapeDtypeStruct((bsz, HEADS, HEAD_DIM, s), bf)
    return pl.pallas_call(
        functools.partial(_attn_kernel, seq=s, topk=topk),
        grid=(bsz, nq),
        in_specs=[qspec, qspec, qspec,
                  pl.BlockSpec((1, HEADS, QB), lambda b, i: (b, 0, i)),
                  pl.BlockSpec((1, PAIRS, s, LANES), lambda b, i: (b, 0, 0, 0)),
                  pl.BlockSpec((1, HEADS, nc, HEAD_DIM, KC), lambda b, i: (b, 0, 0, 0, 0)),
                  pl.BlockSpec((1, s, LANES), lambda b, i: (b, 0, 0)),
                  pl.BlockSpec((1, s, LANES), lambda b, i: (b, 0, 0)),
                  pl.BlockSpec((1, nc, HEAD_DIM, KC), lambda b, i: (b, 0, 0, 0)),
                  const((PAIRS, N_META, LANES)), const((HEADS, HEAD_DIM, LANES)),
                  const((N_META, LANES)), const((N_META, LANES)), const((HEAD_DIM, LANES)),
                  const((HEADS, KC, QB)), const((HEADS, KC, QB)), const((HEADS, N_META, QB))],
        out_specs=[yspec, yspec],
        out_shape=[yshape, yshape],
        scratch_shapes=[pltpu.VMEM((2, HEADS, QB, LANES), bf),
                        pltpu.VMEM((HEADS, QB, LANES), IDX_DTYPE),
                        pltpu.VMEM((s + N_META, QB), jnp.int32),
                        pltpu.VMEM((s + N_META, QB), jnp.float32),
                        pltpu.VMEM((HEADS, HEAD_DIM, QB), jnp.float32),
                        pltpu.VMEM((HEADS, SUBLANES, QB), jnp.float32),
                        pltpu.VMEM((HEADS, HEAD_DIM + DEN_ROWS, QB), jnp.float32),
                        pltpu.VMEM((HEADS, SUBLANES, QB), jnp.float32),
                        pltpu.VMEM((2, HEADS, KC, QB), jnp.float32),
                        pltpu.VMEM((nc, 32, SUBLANES, QB), jnp.int32),
                        pltpu.VMEM((nc, SUBLANES, QB), jnp.int32),
                        pltpu.VMEM((nc, SUBLANES, QB), jnp.int32)],
        compiler_params=pltpu.CompilerParams(
            dimension_semantics=("arbitrary", "arbitrary"), vmem_limit_bytes=VMEM_LIMIT),
        name="attn",
    )(qa, qb, qi, wit, ka, vat, ki, kb, vbt, kam, vatm, kim, kbm, vbtm, bias_d, bias_p, bias_m)


def _sigmoid(x):
    return 1.0 / (1.0 + jnp.exp(-x))


def _merge_kernel(x_ref, ya_ref, yb_ref, g_in_ref, b_in_ref, wz_ref, bg_ref, wpa_ref, wpb_ref,
                  wo_ref, g_ref, b_ref, o_ref):
    for t in range(x_ref.shape[1] // MERGE_ROWS):
        rows = slice(t * MERGE_ROWS, (t + 1) * MERGE_ROWS)
        h = _layer_norm(x_ref[0, rows], g_in_ref[...], b_in_ref[...])
        hb = h.astype(MXU_DTYPE)

        def branch(yt_ref, z_col, g_col, bg_col, wp_ref):
            z = _dot(hb, wz_ref[:, z_col:z_col + WIDTH])
            y = yt_ref[0, :, rows].astype(jnp.float32).T
            y = (y * (z * _sigmoid(z))).astype(MXU_DTYPE)
            gate = _sigmoid(_dot(hb, wz_ref[:, g_col:g_col + D_MODEL]) + bg_ref[:, bg_col:bg_col + D_MODEL])
            return gate * _dot(y, wp_ref[...])

        merged = (branch(ya_ref, 0, 2 * WIDTH, 0, wpa_ref)
                  + branch(yb_ref, WIDTH, 2 * WIDTH + D_MODEL, D_MODEL, wpb_ref))
        out = _dot(merged.astype(MXU_DTYPE), wo_ref[...])
        o_ref[0, rows] = _layer_norm(DEEPNORM_ALPHA * h + out, g_ref[...], b_ref[...])


def _merge_call(x, ya, yb, ln_in_g, ln_in_b, wz, bg, wpa, wpb, wo, ln_g, ln_b, tm):
    bsz, s, d = x.shape
    assert s % tm == 0
    const = lambda shape: pl.BlockSpec(shape, lambda b, i: (0,) * len(shape))
    row = lambda width: pl.BlockSpec((1, tm, width), lambda b, i: (b, i, 0))
    col = pl.BlockSpec((1, WIDTH, tm), lambda b, i: (b, 0, i))
    return pl.pallas_call(
        _merge_kernel,
        grid=(bsz, s // tm),
        in_specs=[row(d), col, col, const((1, d)), const((1, d)),
                  const((d, 2 * WIDTH + 2 * d)), const((1, 2 * d)),
                  const((WIDTH, d)), const((WIDTH, d)), const((d, d)), const((1, d)), const((1, d))],
        out_specs=row(d),
        out_shape=jax.ShapeDtypeStruct((bsz, s, d), jnp.float32),
        compiler_params=pltpu.CompilerParams(
            dimension_semantics=("arbitrary", "arbitrary"), vmem_limit_bytes=VMEM_LIMIT),
        name="merge",
    )(x, ya, yb, ln_in_g, ln_in_b, wz, bg, wpa, wpb, wo, ln_g, ln_b)


def _perm_rows(a, axis, chunk):
    shp = a.shape
    n = shp[axis] // chunk
    a = a.reshape(shp[:axis] + (n, SUBLANES, chunk // SUBLANES) + shp[axis + 1:])
    a = jnp.swapaxes(a, axis + 1, axis + 2)
    return a.reshape(shp)


def _rel_bucket(dist):
    max_exact = REL_BUCKETS // 2
    nf = np.maximum(dist, 1).astype(np.float32)
    large = max_exact + (np.log(nf / np.float32(max_exact)) / np.float32(math.log(REL_MAX_DIST / max_exact))
                         * np.float32(REL_BUCKETS - max_exact)).astype(np.int32)
    large = np.minimum(large, REL_BUCKETS - 1)
    return np.where(dist < max_exact, dist, large)


def _bias_tiles(rel_bias):
    r = np.arange(KC)
    pos = (r % SUBLANES) * (KC // SUBLANES) + r // SUBLANES
    q = np.arange(QB)
    rm = np.arange(N_META)
    posm = (rm % SUBLANES) * (N_META // SUBLANES) + rm // SUBLANES
    table = (rel_bias - rel_bias[REL_BUCKETS - 1]).astype(jnp.float32) * LOG2E

    def tile(dist):
        onehot = jnp.asarray(_rel_bucket(dist)[..., None] == np.arange(REL_BUCKETS), jnp.float32)
        return jnp.einsum("rqb,bh->hrq", onehot, table, precision=lax.Precision.HIGHEST)

    return (tile(np.maximum(q[None, :] - pos[:, None], 0)),
            tile(KC + q[None, :] - pos[:, None]),
            tile(N_META + q[None, :] - posm[:, None]))


def kernel(x, meta_tokens, ln_in_g, ln_in_b, rel_bias, w_in, b_gate, idx_kn_g, idx_kn_b,
           w_pa, w_pb, w_o, ln_g, ln_b):
    bsz, seq, d = x.shape
    assert d == D_MODEL and seq % KC == 0 and w_in.shape[0] == DEPTH == 1
    topk = min(TOPK_MAX, seq // 4)
    nc = seq // KC
    bf = MXU_DTYPE
    w = w_in[0]
    dup = lambda a: jnp.concatenate([a, a], axis=-1)
    cols = lambda off, width: w[:, off:off + width]
    w_q = jnp.concatenate([cols(O_QA, WIDTH), cols(O_QB, WIDTH)], axis=1).astype(bf)
    w_qi = jnp.concatenate([cols(O_QI, WIDTH), jnp.repeat(cols(O_WI, HEADS), HEAD_DIM, axis=1)],
                           axis=1).astype(IDX_DTYPE)
    w_k = jnp.concatenate([cols(O_KA, WIDTH), dup(cols(O_KB, HEAD_DIM))], axis=1).astype(bf)
    w_ki = dup(cols(O_KI, HEAD_DIM)).astype(IDX_DTYPE)
    w_vt = jnp.concatenate([cols(O_VA, WIDTH), cols(O_VB, HEAD_DIM)], axis=1).T.astype(bf)
    w_wit = cols(O_WI, HEADS).T.astype(IDX_DTYPE)
    w_z = jnp.concatenate([cols(O_ZA, WIDTH), cols(O_ZB, WIDTH), cols(O_GA, D_MODEL), cols(O_GB, D_MODEL)],
                          axis=1).astype(bf)
    row = lambda a: a.reshape(1, -1).astype(jnp.float32)
    r = np.arange(KC)
    perm = np.zeros((KC, KC), np.float32)
    perm[r, (r % SUBLANES) * (KC // SUBLANES) + r // SUBLANES] = 1.0

    proj = functools.partial(_proj_call, ln_g=row(ln_in_g), ln_b=row(ln_in_b), wq=w_q, wqi=w_qi, wk=w_k,
                             wki=w_ki, wvt=w_vt, wit=w_wit,
                             kn_g2=row(dup(idx_kn_g[0])), kn_b2=row(dup(idx_kn_b[0])))
    qa, qb, qi, wit, ka, ki, kb, vat, vbt = proj(x, perm=jnp.asarray(perm, IDX_DTYPE),
                                                 step_chunks=math.gcd(nc, PROJ_CHUNKS))
    x_meta = jnp.pad(meta_tokens.astype(x.dtype), ((0, KC - N_META), (0, 0)))[None]
    _, _, _, _, kam, kim, kbm, vatm, vbtm = proj(x_meta, perm=jnp.eye(KC, dtype=IDX_DTYPE), step_chunks=1)
    pad_keys = lambda a: jnp.pad(a, [(0, 0)] * (a.ndim - 1) + [(0, LANES - N_META)])
    kam = _perm_rows(kam[0, :, :N_META], 1, N_META)
    kim = _perm_rows(kim[0, :N_META], 0, N_META)
    kbm = _perm_rows(kbm[0, :N_META], 0, N_META)
    vatm = pad_keys(_perm_rows(vatm[0, :, 0, :, :N_META], 2, N_META))
    vbtm = pad_keys(_perm_rows(vbtm[0, 0, :, :N_META], 1, N_META))
    bias_d, bias_p, bias_m = _bias_tiles(rel_bias)

    yat, ybt = _attn_call(qa, qb, qi, wit, ka, vat, ki, kb, vbt, kam, vatm, kim, kbm, vbtm,
                          bias_d, bias_p, bias_m, topk)
    ya = yat.reshape(bsz, WIDTH, seq)
    yb = ybt.reshape(bsz, WIDTH, seq)

    return _merge_call(x, ya, yb, row(ln_in_g), row(ln_in_b), w_z, row(b_gate[0]),
                       w_pa[0].astype(bf), w_pb[0].astype(bf), w_o[0].astype(bf),
                       row(ln_g[0]), row(ln_b[0]), tm=KC * math.gcd(nc, MERGE_CHUNKS))
```

```python
import functools
import math

import jax
import jax.numpy as jnp
import numpy as np
from jax import lax
from jax.experimental import pallas as pl
from jax.experimental.pallas import tpu as pltpu

D_MODEL = 1024
DEPTH = 1
N_META = 16
HEADS = 8
HEAD_DIM = 64
WIDTH = HEADS * HEAD_DIM
TOPK_MAX = 256
REL_BUCKETS = 32
REL_MAX_DIST = 128
LN_EPS = 1e-5
DEEPNORM_ALPHA = (2.0 * DEPTH) ** 0.25
IDX_SCALE = HEADS ** -0.5 * HEAD_DIM ** -0.5
QK_SCALE = HEAD_DIM ** -0.5
LOG2E = math.log2(math.e)
QK_SCALE_LOG2 = QK_SCALE * LOG2E

SUBLANES = 8
LANES = 128
QB = 256
KC = 256
PAIRS = HEADS // 2
PROJ_CHUNKS = 4
MERGE_CHUNKS = 4
MERGE_ROWS = 256
DEN_ROWS = 16
NEG = -1e30
SB_DONE_BITS = 160.0
INT_MIN = -2 ** 31
VMEM_LIMIT = 56 * 1024 * 1024
MXU_DTYPE = jnp.bfloat16
IDX_DTYPE = jnp.float32

_COLS = [WIDTH, WIDTH, WIDTH, WIDTH, WIDTH, HEAD_DIM, HEAD_DIM, WIDTH, WIDTH, HEAD_DIM, HEADS,
         D_MODEL, D_MODEL]
_OFF = [0]
for _c in _COLS:
    _OFF.append(_OFF[-1] + _c)
(O_QA, O_KA, O_VA, O_ZA, O_QB, O_KB, O_VB, O_ZB, O_QI, O_KI, O_WI, O_GA, O_GB, O_END) = _OFF


def _layer_norm(x, g, b):
    mu = jnp.mean(x, axis=-1, keepdims=True)
    xc = x - mu
    var = jnp.mean(xc * xc, axis=-1, keepdims=True)
    return xc * lax.rsqrt(var + LN_EPS) * g + b


def _dot(a, b):
    return jnp.dot(a, b, preferred_element_type=jnp.float32)


def _dot_nt(a, b):
    return lax.dot_general(a, b, (((1,), (1,)), ((), ())), preferred_element_type=jnp.float32)


WQ_COLS = 2 * WIDTH
WQI_COLS = 2 * WIDTH
WK_COLS = WIDTH + LANES
WV_ROWS = WIDTH + HEAD_DIM


def _proj_kernel(x_ref, g_ref, b_ref, perm_ref, wq_ref, wqi_ref, wk_ref, wki_ref, wvt_ref, wit_ref,
                 kng_ref, knb_ref,
                 qa_ref, qb_ref, qi_ref, wi_ref, ka_ref, ki_ref, kb_ref, vat_ref, vbt_ref):
    for c in range(x_ref.shape[1] // KC):
        rows = slice(c * KC, (c + 1) * KC)
        hi = _layer_norm(x_ref[0, rows], g_ref[...], b_ref[...]).astype(IDX_DTYPE)
        h = hi.astype(MXU_DTYPE)
        hp = _dot(perm_ref[...].astype(MXU_DTYPE), h).astype(MXU_DTYPE)
        hpi = _dot(perm_ref[...], hi).astype(IDX_DTYPE)

        def store_pairs(ref, val):
            for pr in range(PAIRS):
                ref[0, pr, rows] = val[:, pr * LANES:(pr + 1) * LANES].astype(ref.dtype)

        store_pairs(qa_ref, _dot(h, wq_ref[:, 0:WIDTH]) * QK_SCALE_LOG2)
        store_pairs(qb_ref, _dot(h, wq_ref[:, WIDTH:WQ_COLS]) * QK_SCALE_LOG2)
        store_pairs(qi_ref, _dot(hi, wqi_ref[:, 0:WIDTH]) * _dot(hi, wqi_ref[:, WIDTH:WQI_COLS]) * IDX_SCALE)
        wi_ref[0, :, rows] = _dot_nt(wit_ref[...], hi)

        store_pairs(ka_ref, _dot(hp, wk_ref[:, 0:WIDTH]))
        kb_ref[0, rows] = _dot(hp, wk_ref[:, WIDTH:WK_COLS]).astype(kb_ref.dtype)
        ki_ref[0, rows] = _layer_norm(_dot(hpi, wki_ref[...]), kng_ref[...], knb_ref[...]).astype(ki_ref.dtype)
        vat = _dot_nt(wvt_ref[0:WIDTH, :], hp)
        vat_ref[0, :, c] = vat.reshape(HEADS, HEAD_DIM, KC).astype(vat_ref.dtype)
        vbt_ref[0, c] = _dot_nt(wvt_ref[WIDTH:WV_ROWS, :], hp).astype(vbt_ref.dtype)


def _proj_call(x, ln_g, ln_b, perm, wq, wqi, wk, wki, wvt, wit, kn_g2, kn_b2, step_chunks):
    bsz, s, d = x.shape
    rows = step_chunks * KC
    assert s % rows == 0 and perm.shape == (KC, KC)
    nc = s // KC
    bf = MXU_DTYPE
    pair_shape = lambda dt: jax.ShapeDtypeStruct((bsz, PAIRS, s, LANES), dt)
    pair_spec = pl.BlockSpec((1, PAIRS, rows, LANES), lambda b, i: (b, 0, i, 0))
    row128 = pl.BlockSpec((1, rows, LANES), lambda b, i: (b, i, 0))
    row_shape = lambda dt: jax.ShapeDtypeStruct((bsz, s, LANES), dt)
    const = lambda shape: pl.BlockSpec(shape, lambda b, i: (0,) * len(shape))
    return pl.pallas_call(
        _proj_kernel,
        grid=(bsz, s // rows),
        in_specs=[pl.BlockSpec((1, rows, d), lambda b, i: (b, i, 0)),
                  const((1, d)), const((1, d)), const((KC, KC)),
                  const((d, WQ_COLS)), const((d, WQI_COLS)), const((d, WK_COLS)), const((d, LANES)),
                  const((WV_ROWS, d)), const((HEADS, d)),
                  const((1, LANES)), const((1, LANES))],
        out_specs=[pair_spec, pair_spec, pair_spec,
                   pl.BlockSpec((1, HEADS, rows), lambda b, i: (b, 0, i)),
                   pair_spec, row128, row128,
                   pl.BlockSpec((1, HEADS, rows // KC, HEAD_DIM, KC), lambda b, i: (b, 0, i, 0, 0)),
                   pl.BlockSpec((1, rows // KC, HEAD_DIM, KC), lambda b, i: (b, i, 0, 0))],
        out_shape=[pair_shape(bf), pair_shape(bf), pair_shape(IDX_DTYPE),
                   jax.ShapeDtypeStruct((bsz, HEADS, s), jnp.float32),
                   pair_shape(bf), row_shape(IDX_DTYPE), row_shape(bf),
                   jax.ShapeDtypeStruct((bsz, HEADS, nc, HEAD_DIM, KC), bf),
                   jax.ShapeDtypeStruct((bsz, nc, HEAD_DIM, KC), bf)],
        compiler_params=pltpu.CompilerParams(
            dimension_semantics=("arbitrary", "arbitrary"), vmem_limit_bytes=VMEM_LIMIT),
        name="proj",
    )(x, ln_g, ln_b, perm, wq, wqi, wk, wki, wvt, wit, kn_g2, kn_b2)


def _group_excl_scan(g_tot, reverse):
    sub = lax.broadcasted_iota(jnp.int32, g_tot.shape, 0)
    out = jnp.zeros_like(g_tot)
    for g in range(SUBLANES):
        row = g_tot[g:g + 1, :]
        take = (sub < g) if reverse else (sub > g)
        out = out + jnp.where(take, row, 0.0)
    return out


def _bit_planes(key):
    assert key.shape[0] == 32 * SUBLANES
    k3 = key.reshape(32, SUBLANES, key.shape[1])
    a = [k3[j] ^ jnp.int32(INT_MIN) for j in range(32)]
    as_i32 = lambda m: jnp.int32(m - (1 << 32) if m >= (1 << 31) else m)
    j, m = 16, 0x0000FFFF
    while j:
        shift = jnp.full(a[0].shape, j, jnp.int32)
        k = 0
        while k < 32:
            t = (a[k] ^ lax.shift_right_logical(a[k + j], shift)) & as_i32(m)
            a[k] = a[k] ^ t
            a[k + j] = a[k + j] ^ lax.shift_left(t, shift)
            k = (k + j + 1) & ~j
        j >>= 1
        m = (m ^ (m << j)) & 0xFFFFFFFF
    return a


def _attn_kernel(qa_ref, qb_ref, qi_ref, wi_ref,
                 ka_ref, vat_ref, ki_ref, kb_ref, vbt_ref,
                 kam_ref, vatm_ref, kim_ref, kbm_ref, vbtm_ref,
                 biasd_ref, biasp_ref, biasm_ref,
                 ya_ref, yb_ref,
                 qpad_scr, qipad_scr, key_scr, mb_scr, sbacc_scr, sbcar_scr, dacc_scr, m_scr, z_scr,
                 plane_scr, alive_scr, ones_scr, *, seq, topk):
    i = pl.program_id(1)
    nslab = KC // SUBLANES
    mslab = N_META // SUBLANES
    f32 = jnp.float32

    lane = lax.broadcasted_iota(jnp.int32, (QB, LANES), 1)
    lo_half = (lane < HEAD_DIM).astype(f32)
    hi_half = (lane >= HEAD_DIM).astype(f32)
    for pr in range(PAIRS):
        for t, ref in enumerate((qa_ref, qb_ref)):
            qp = ref[0, pr]
            qpad_scr[t, 2 * pr] = qp * lo_half.astype(qp.dtype)
            qpad_scr[t, 2 * pr + 1] = qp * hi_half.astype(qp.dtype)
        qp = qi_ref[0, pr]
        qipad_scr[2 * pr] = qp * lo_half.astype(qp.dtype)
        qipad_scr[2 * pr + 1] = qp * hi_half.astype(qp.dtype)

    r_io = lax.broadcasted_iota(jnp.int32, (KC, QB), 0)
    c_io = lax.broadcasted_iota(jnp.int32, (KC, QB), 1)
    pos = (r_io & (SUBLANES - 1)) * nslab + (r_io >> 3)
    mask_lt = pos < c_io
    mask_le = pos <= c_io

    diag0 = pl.multiple_of(i * KC, KC)

    def issue(*streams):
        for logits_of_head, _, rows, branch in streams:
            for h in range(HEADS):
                z_scr[branch, h, 0:rows] = logits_of_head(h)

    def finish(*streams):
        for h in range(HEADS):
            for _, rest_of_head, rows, branch in streams:
                rest_of_head(h, z_scr[branch, h, 0:rows])()

    def sb_chunk(k_of_pair, vt_of_head, mask, n_j, pad_rows, first):
        def logits(h):
            return _dot_nt(k_of_pair(h // 2), qpad_scr[0, h])

        def rest(h, z):
            sp = jnp.maximum(z, 0.0) + jnp.log2(1.0 + jnp.exp2(-jnp.abs(z)))
            if mask is not None:
                sp = jnp.where(mask, sp, 0.0)
            sp3 = sp.reshape(n_j, SUBLANES, QB)
            z3 = z.reshape(n_j, SUBLANES, QB)
            run = jnp.zeros((SUBLANES, QB), f32)
            u = [None] * n_j
            for j in reversed(range(n_j)):
                run = run + sp3[j]
                u[j] = jnp.exp2(z3[j] - run)
            carry = jnp.zeros((SUBLANES, QB), f32) if first else sbcar_scr[h]
            scale = jnp.exp2(-(_group_excl_scan(run, reverse=True) + carry))
            a = jnp.concatenate([u[j] * scale for j in range(n_j)], axis=0)
            if mask is not None:
                a = jnp.where(mask, a, 0.0)
            a = a.astype(MXU_DTYPE)
            if pad_rows:
                a = jnp.concatenate([a, jnp.zeros((pad_rows, QB), MXU_DTYPE)], axis=0)
            sbcar_scr[h] = carry + jnp.sum(run, axis=0, keepdims=True)

            def finish():
                pv = _dot(vt_of_head(h), a)
                sbacc_scr[h] = pv if first else sbacc_scr[h] + pv

            return finish

        return logits, rest, n_j * SUBLANES, 0

    def sb_real_chunk(c, mask, first):
        r0 = pl.multiple_of(c * KC, KC)
        return sb_chunk(lambda pr: ka_ref[0, pr, pl.ds(r0, KC), :], lambda h: vat_ref[0, h, c],
                        mask, nslab, 0, first)

    w = wi_ref[0]
    lo = jnp.where(w >= 0.0, 0.0, -jnp.inf)
    hi = jnp.where(w >= 0.0, jnp.inf, 0.0)

    def score_keys(kic, mask):
        acc = jnp.zeros((kic.shape[0], QB), f32)
        for h in range(HEADS):
            s = _dot_nt(kic, qipad_scr[h])
            acc = acc + jnp.minimum(jnp.maximum(s, lo[h:h + 1, :]), hi[h:h + 1, :])
        if mask is not None:
            acc = jnp.where(mask, acc, -jnp.inf)
        bits = lax.bitcast_convert_type(acc, jnp.int32)
        key = jnp.where(bits < 0, bits ^ jnp.int32(0x7FFFFFFF), bits)
        return jnp.where(bits == jnp.int32(INT_MIN), 0, key)

    nchunks = seq // KC

    @pl.when(i == 0)
    def _():
        for c in range(1, nchunks):
            plane_scr[c] = jnp.zeros((32, SUBLANES, QB), jnp.int32)

    def store_chunk_keys(c, r0, mask):
        key = score_keys(ki_ref[0, pl.ds(r0, KC), :], mask)
        key_scr[pl.ds(r0, KC), :] = key
        planes = _bit_planes(key)
        for o in range(32):
            plane_scr[c, o] = planes[o]

    def score_body(c, _):
        store_chunk_keys(c, pl.multiple_of(c * KC, KC), None)
        return 0

    lax.fori_loop(0, i, score_body, 0)
    store_chunk_keys(i, diag0, mask_le)
    meta_keys = score_keys(kim_ref[...], None)
    key_scr[seq:seq + N_META, :] = meta_keys

    def count_meta(cand, strict):
        cb = jnp.broadcast_to(cand, (N_META, QB))
        hit = (meta_keys > cb) if strict else (meta_keys >= cb)
        return jnp.sum(jnp.where(hit, 1.0, 0.0), axis=0, keepdims=True)

    def radix_step(it, st):
        ans, above, took = st
        cand = ans ^ lax.shift_left(jnp.int32(1), 31 - it)
        took_b = jnp.broadcast_to(took, (SUBLANES, QB)) != 0
        acc = jnp.zeros((SUBLANES, QB), jnp.int32)
        for c in range(nchunks):
            ones = ones_scr[c]
            alive = jnp.where(took_b, ones, alive_scr[c] ^ ones)
            ones = alive & plane_scr[c, it]
            alive_scr[c] = alive
            ones_scr[c] = ones
            acc = acc + lax.population_count(ones)
        ones_cnt = jnp.sum(acc.astype(f32), axis=0, keepdims=True)
        take = (above + ones_cnt + count_meta(cand, False)) >= float(topk)
        return (jnp.where(take, cand, ans), jnp.where(take, above, above + ones_cnt),
                jnp.where(take, -1, 0).astype(jnp.int32))

    for c in range(nchunks):
        alive_scr[c] = jnp.full((SUBLANES, QB), -1, jnp.int32)
        ones_scr[c] = jnp.full((SUBLANES, QB), -1, jnp.int32)

    thr, above, _ = lax.fori_loop(
        0, 32, radix_step,
        (jnp.full((1, QB), INT_MIN, jnp.int32), jnp.zeros((1, QB), f32), jnp.full((1, QB), -1, jnp.int32)))

    need = float(topk) - (above + count_meta(thr, True))
    thr_b = jnp.broadcast_to(thr, (SUBLANES, QB))
    need_b = jnp.broadcast_to(need, (SUBLANES, QB))

    def select_rows(blk, n_j, mask, carry):
        b3 = blk.reshape(n_j, SUBLANES, QB)
        tie = [jnp.where(b3[j] == thr_b, 1.0, 0.0) for j in range(n_j)]
        run = jnp.zeros((SUBLANES, QB), f32)
        prefix = []
        for j in range(n_j):
            run = run + tie[j]
            prefix.append(run)
        off = _group_excl_scan(run, reverse=False) + carry
        rows = []
        for j in range(n_j):
            rank = prefix[j] - tie[j] + off
            sel = (b3[j] > thr_b) | ((tie[j] > 0.0) & (rank < need_b))
            rows.append(jnp.where(sel, 0.0, NEG))
        mb = jnp.concatenate(rows, axis=0)
        if mask is not None:
            mb = jnp.where(mask, mb, NEG)
        return mb, carry + jnp.sum(run, axis=0, keepdims=True)

    mb, tie_carry = select_rows(key_scr[seq:seq + N_META, :], mslab, None, jnp.zeros((1, QB), f32))
    mb_scr[seq:seq + N_META, :] = mb

    def select_body(c, carry):
        r0 = pl.multiple_of(c * KC, KC)
        mb_c, carry = select_rows(key_scr[pl.ds(r0, KC), :], nslab, None, carry)
        mb_scr[pl.ds(r0, KC), :] = mb_c
        return carry

    tie_carry = lax.fori_loop(0, i, select_body, tie_carry)
    mb, _ = select_rows(key_scr[pl.ds(diag0, KC), :], nslab, mask_le, tie_carry)
    mb_scr[pl.ds(diag0, KC), :] = mb

    def dsa_chunk(kc_fn, vt_fn, mbias_fn, bias_of_head, rows, pad_rows, first, last, buf=1):
        def logits(h):
            return _dot_nt(kc_fn(), qpad_scr[1, h])

        def rest(h, qk):
            lg = qk + mbias_fn()
            if bias_of_head is not None:
                lg = lg + bias_of_head(h)
            cmax = jnp.max(lg, axis=0, keepdims=True)
            if first:
                m_new = jnp.broadcast_to(cmax, (SUBLANES, QB))
            else:
                m_old = m_scr[h]
                m_new = jnp.maximum(m_old, cmax)
                alpha = jnp.exp2(m_old - m_new)
            p3 = lg.reshape(lg.shape[0] // SUBLANES, SUBLANES, QB)
            pb = jnp.exp2(p3 - m_new[None]).reshape(lg.shape).astype(MXU_DTYPE)
            if pad_rows:
                pb = jnp.concatenate([pb, jnp.zeros((pad_rows, QB), MXU_DTYPE)], axis=0)
            if not last:
                m_scr[h] = m_new

            def finish():
                vt = vt_fn()
                vt_ones = jnp.concatenate([vt, jnp.ones((DEN_ROWS, vt.shape[1]), vt.dtype)], axis=0)
                pv = _dot(vt_ones, pb)
                acc = pv if first else dacc_scr[h] * alpha[0:1, :] + pv
                if last:
                    yb_ref[0, h] = (acc[0:HEAD_DIM] / acc[HEAD_DIM:HEAD_DIM + 1]).astype(yb_ref.dtype)
                else:
                    dacc_scr[h] = acc

            return finish

        return logits, rest, rows, buf

    def dsa_real_chunk(c, bias_of_head, first):
        r0 = pl.multiple_of(c * KC, KC)
        return dsa_chunk(lambda: kb_ref[0, pl.ds(r0, KC), :], lambda: vbt_ref[0, c],
                         lambda: mb_scr[pl.ds(r0, KC), :], bias_of_head, KC, 0, first, False)

    def run_pair(*streams):
        issue(*streams)
        finish(*streams)

    def pair(p, bias_of_head=None):
        return sb_real_chunk(i - 1 - p, None, False), dsa_real_chunk(p, bias_of_head, False)

    def sb_open():
        return jnp.min(jnp.min(sbcar_scr[...], axis=0)) < SB_DONE_BITS

    run_pair(sb_real_chunk(i, mask_lt, True), dsa_real_chunk(i, lambda h: biasd_ref[h], True))

    def both_cond(st):
        p, still_open = st
        return (p < i - 1) & still_open

    def both_body(st):
        run_pair(*pair(st[0]))
        return st[0] + 1, sb_open()

    p_sparse, _ = lax.while_loop(both_cond, both_body, (jnp.int32(0), sb_open()))

    def sparse_only(p, _):
        run_pair(dsa_real_chunk(p, None, False))
        return 0

    lax.fori_loop(p_sparse, i - 1, sparse_only, 0)

    prev_bias = lambda h: biasp_ref[h]
    open_prev = sb_open()

    @pl.when((i > 0) & open_prev)
    def _():
        run_pair(*pair(i - 1, prev_bias))

    first_block = (i == 0).astype(f32)
    sb_meta = sb_chunk(lambda pr: kam_ref[pr], lambda h: vatm_ref[h], None, mslab, LANES - N_META, False)

    def dsa_meta(buf):
        return dsa_chunk(lambda: kbm_ref[...], lambda: vbtm_ref[...], lambda: mb_scr[seq:seq + N_META, :],
                         lambda h: biasm_ref[h] * first_block, N_META, LANES - N_META, False, True, buf)

    closed_prev = (i > 0) & jnp.logical_not(open_prev)

    @pl.when(closed_prev)
    def _():
        run_pair(dsa_real_chunk(i - 1, prev_bias, False), dsa_meta(0))

    open_meta = sb_open()

    @pl.when(jnp.logical_not(closed_prev) & open_meta)
    def _():
        run_pair(sb_meta, dsa_meta(1))

    @pl.when(jnp.logical_not(closed_prev) & jnp.logical_not(open_meta))
    def _():
        run_pair(dsa_meta(1))

    for h in range(HEADS):
        ya_ref[0, h] = sbacc_scr[h].astype(ya_ref.dtype)


def _attn_call(qa, qb, qi, wit, ka, vat, ki, kb, vbt, kam, vatm, kim, kbm, vbtm,
               bias_d, bias_p, bias_m, topk):
    bsz, _, s, _ = qa.shape
    nq = s // QB
    nc = s // KC
    bf = MXU_DTYPE
    qspec = pl.BlockSpec((1, PAIRS, QB, LANES), lambda b, i: (b, 0, i, 0))
    const = lambda shape: pl.BlockSpec(shape, lambda b, i: (0,) * len(shape))
    yspec = pl.BlockSpec((1, HEADS, HEAD_DIM, QB), lambda b, i: (b, 0, 0, i))
    yshape = jax.ShapeDtypeStruct((bsz, HEADS, HEAD_DIM, s), bf)
    return pl.pallas_call(
        functools.partial(_attn_kernel, seq=s, topk=topk),
        grid=(bsz, nq),
        in_specs=[qspec, qspec, qspec,
                  pl.BlockSpec((1, HEADS, QB), lambda b, i: (b, 0, i)),
                  pl.BlockSpec((1, PAIRS, s, LANES), lambda b, i: (b, 0, 0, 0)),
                  pl.BlockSpec((1, HEADS, nc, HEAD_DIM, KC), lambda b, i: (b, 0, 0, 0, 0)),
                  pl.BlockSpec((1, s, LANES), lambda b, i: (b, 0, 0)),
                  pl.BlockSpec((1, s, LANES), lambda b, i: (b, 0, 0)),
                  pl.BlockSpec((1, nc, HEAD_DIM, KC), lambda b, i: (b, 0, 0, 0)),
                  const((PAIRS, N_META, LANES)), const((HEADS, HEAD_DIM, LANES)),
                  const((N_META, LANES)), const((N_META, LANES)), const((HEAD_DIM, LANES)),
                  const((HEADS, KC, QB)), const((HEADS, KC, QB)), const((HEADS, N_META, QB))],
        out_specs=[yspec, yspec],
        out_shape=[yshape, yshape],
        scratch_shapes=[pltpu.VMEM((2, HEADS, QB, LANES), bf),
                        pltpu.VMEM((HEADS, QB, LANES), IDX_DTYPE),
                        pltpu.VMEM((s + N_META, QB), jnp.int32),
                        pltpu.VMEM((s + N_META, QB), jnp.float32),
                        pltpu.VMEM((HEADS, HEAD_DIM, QB), jnp.float32),
                        pltpu.VMEM((HEADS, SUBLANES, QB), jnp.float32),
                        pltpu.VMEM((HEADS, HEAD_DIM + DEN_ROWS, QB), jnp.float32),
                        pltpu.VMEM((HEADS, SUBLANES, QB), jnp.float32),
                        pltpu.VMEM((2, HEADS, KC, QB), jnp.float32),
                        pltpu.VMEM((nc, 32, SUBLANES, QB), jnp.int32),
                        pltpu.VMEM((nc, SUBLANES, QB), jnp.int32),
                        pltpu.VMEM((nc, SUBLANES, QB), jnp.int32)],
        compiler_params=pltpu.CompilerParams(
            dimension_semantics=("arbitrary", "arbitrary"), vmem_limit_bytes=VMEM_LIMIT),
        name="attn",
    )(qa, qb, qi, wit, ka, vat, ki, kb, vbt, kam, vatm, kim, kbm, vbtm, bias_d, bias_p, bias_m)


def _sigmoid(x):
    return 1.0 / (1.0 + jnp.exp(-x))


def _merge_kernel(x_ref, ya_ref, yb_ref, g_in_ref, b_in_ref, wz_ref, bg_ref, wpa_ref, wpb_ref,
                  wo_ref, g_ref, b_ref, o_ref):
    for t in range(x_ref.shape[1] // MERGE_ROWS):
        rows = slice(t * MERGE_ROWS, (t + 1) * MERGE_ROWS)
        h = _layer_norm(x_ref[0, rows], g_in_ref[...], b_in_ref[...])
        hb = h.astype(MXU_DTYPE)

        def branch(yt_ref, z_col, g_col, bg_col, wp_ref):
            z = _dot(hb, wz_ref[:, z_col:z_col + WIDTH])
            y = yt_ref[0, :, rows].astype(jnp.float32).T
            y = (y * (z * _sigmoid(z))).astype(MXU_DTYPE)
            gate = _sigmoid(_dot(hb, wz_ref[:, g_col:g_col + D_MODEL]) + bg_ref[:, bg_col:bg_col + D_MODEL])
            return gate * _dot(y, wp_ref[...])

        merged = (branch(ya_ref, 0, 2 * WIDTH, 0, wpa_ref)
                  + branch(yb_ref, WIDTH, 2 * WIDTH + D_MODEL, D_MODEL, wpb_ref))
        out = _dot(merged.astype(MXU_DTYPE), wo_ref[...])
        o_ref[0, rows] = _layer_norm(DEEPNORM_ALPHA * h + out, g_ref[...], b_ref[...])


def _merge_call(x, ya, yb, ln_in_g, ln_in_b, wz, bg, wpa, wpb, wo, ln_g, ln_b, tm):
    bsz, s, d = x.shape
    assert s % tm == 0
    const = lambda shape: pl.BlockSpec(shape, lambda b, i: (0,) * len(shape))
    row = lambda width: pl.BlockSpec((1, tm, width), lambda b, i: (b, i, 0))
    col = pl.BlockSpec((1, WIDTH, tm), lambda b, i: (b, 0, i))
    return pl.pallas_call(
        _merge_kernel,
        grid=(bsz, s // tm),
        in_specs=[row(d), col, col, const((1, d)), const((1, d)),
                  const((d, 2 * WIDTH + 2 * d)), const((1, 2 * d)),
                  const((WIDTH, d)), const((WIDTH, d)), const((d, d)), const((1, d)), const((1, d))],
        out_specs=row(d),
        out_shape=jax.ShapeDtypeStruct((bsz, s, d), jnp.float32),
        compiler_params=pltpu.CompilerParams(
            dimension_semantics=("arbitrary", "arbitrary"), vmem_limit_bytes=VMEM_LIMIT),
        name="merge",
    )(x, ya, yb, ln_in_g, ln_in_b, wz, bg, wpa, wpb, wo, ln_g, ln_b)


def _perm_rows(a, axis, chunk):
    shp = a.shape
    n = shp[axis] // chunk
    a = a.reshape(shp[:axis] + (n, SUBLANES, chunk // SUBLANES) + shp[axis + 1:])
    a = jnp.swapaxes(a, axis + 1, axis + 2)
    return a.reshape(shp)


def _rel_bucket(dist):
    max_exact = REL_BUCKETS // 2
    nf = np.maximum(dist, 1).astype(np.float32)
    large = max_exact + (np.log(nf / np.float32(max_exact)) / np.float32(math.log(REL_MAX_DIST / max_exact))
                         * np.float32(REL_BUCKETS - max_exact)).astype(np.int32)
    large = np.minimum(large, REL_BUCKETS - 1)
    return np.where(dist < max_exact, dist, large)


def _bias_tiles(rel_bias):
    r = np.arange(KC)
    pos = (r % SUBLANES) * (KC // SUBLANES) + r // SUBLANES
    q = np.arange(QB)
    rm = np.arange(N_META)
    posm = (rm % SUBLANES) * (N_META // SUBLANES) + rm // SUBLANES
    table = (rel_bias - rel_bias[REL_BUCKETS - 1]).astype(jnp.float32) * LOG2E

    def tile(dist):
        onehot = jnp.asarray(_rel_bucket(dist)[..., None] == np.arange(REL_BUCKETS), jnp.float32)
        return jnp.einsum("rqb,bh->hrq", onehot, table, precision=lax.Precision.HIGHEST)

    return (tile(np.maximum(q[None, :] - pos[:, None], 0)),
            tile(KC + q[None, :] - pos[:, None]),
            tile(N_META + q[None, :] - posm[:, None]))


def kernel(x, meta_tokens, ln_in_g, ln_in_b, rel_bias, w_in, b_gate, idx_kn_g, idx_kn_b,
           w_pa, w_pb, w_o, ln_g, ln_b):
    bsz, seq, d = x.shape
    assert d == D_MODEL and seq % KC == 0 and w_in.shape[0] == DEPTH == 1
    topk = min(TOPK_MAX, seq // 4)
    nc = seq // KC
    bf = MXU_DTYPE
    w = w_in[0]
    dup = lambda a: jnp.concatenate([a, a], axis=-1)
    cols = lambda off, width: w[:, off:off + width]
    w_q = jnp.concatenate([cols(O_QA, WIDTH), cols(O_QB, WIDTH)], axis=1).astype(bf)
    w_qi = jnp.concatenate([cols(O_QI, WIDTH), jnp.repeat(cols(O_WI, HEADS), HEAD_DIM, axis=1)],
                           axis=1).astype(IDX_DTYPE)
    w_k = jnp.concatenate([cols(O_KA, WIDTH), dup(cols(O_KB, HEAD_DIM))], axis=1).astype(bf)
    w_ki = dup(cols(O_KI, HEAD_DIM)).astype(IDX_DTYPE)
    w_vt = jnp.concatenate([cols(O_VA, WIDTH), cols(O_VB, HEAD_DIM)], axis=1).T.astype(bf)
    w_wit = cols(O_WI, HEADS).T.astype(IDX_DTYPE)
    w_z = jnp.concatenate([cols(O_ZA, WIDTH), cols(O_ZB, WIDTH), cols(O_GA, D_MODEL), cols(O_GB, D_MODEL)],
                          axis=1).astype(bf)
    row = lambda a: a.reshape(1, -1).astype(jnp.float32)
    r = np.arange(KC)
    perm = np.zeros((KC, KC), np.float32)
    perm[r, (r % SUBLANES) * (KC // SUBLANES) + r // SUBLANES] = 1.0

    proj = functools.partial(_proj_call, ln_g=row(ln_in_g), ln_b=row(ln_in_b), wq=w_q, wqi=w_qi, wk=w_k,
                             wki=w_ki, wvt=w_vt, wit=w_wit,
                             kn_g2=row(dup(idx_kn_g[0])), kn_b2=row(dup(idx_kn_b[0])))
    qa, qb, qi, wit, ka, ki, kb, vat, vbt = proj(x, perm=jnp.asarray(perm, IDX_DTYPE),
                                                 step_chunks=math.gcd(nc, PROJ_CHUNKS))
    x_meta = jnp.pad(meta_tokens.astype(x.dtype), ((0, KC - N_META), (0, 0)))[None]
    _, _, _, _, kam, kim, kbm, vatm, vbtm = proj(x_meta, perm=jnp.eye(KC, dtype=IDX_DTYPE), step_chunks=1)
    pad_keys = lambda a: jnp.pad(a, [(0, 0)] * (a.ndim - 1) + [(0, LANES - N_META)])
    kam = _perm_rows(kam[0, :, :N_META], 1, N_META)
    kim = _perm_rows(kim[0, :N_META], 0, N_META)
    kbm = _perm_rows(kbm[0, :N_META], 0, N_META)
    vatm = pad_keys(_perm_rows(vatm[0, :, 0, :, :N_META], 2, N_META))
    vbtm = pad_keys(_perm_rows(vbtm[0, 0, :, :N_META], 1, N_META))
    bias_d, bias_p, bias_m = _bias_tiles(rel_bias)

    yat, ybt = _attn_call(qa, qb, qi, wit, ka, vat, ki, kb, vbt, kam, vatm, kim, kbm, vbtm,
                          bias_d, bias_p, bias_m, topk)
    ya = yat.reshape(bsz, WIDTH, seq)
    yb = ybt.reshape(bsz, WIDTH, seq)

    return _merge_call(x, ya, yb, row(ln_in_g), row(ln_in_b), w_z, row(b_gate[0]),
                       w_pa[0].astype(bf), w_pb[0].astype(bf), w_o[0].astype(bf),
                       row(ln_g[0]), row(ln_b[0]), tm=KC * math.gcd(nc, MERGE_CHUNKS))
```

```python
import functools
import math

import jax
import jax.numpy as jnp
import numpy as np
from jax import lax
from jax.experimental import pallas as pl
from jax.experimental.pallas import tpu as pltpu

D_MODEL = 1024
DEPTH = 1
N_META = 16
HEADS = 8
HEAD_DIM = 64
WIDTH = HEADS * HEAD_DIM
TOPK_MAX = 256
REL_BUCKETS = 32
REL_MAX_DIST = 128
LN_EPS = 1e-5
DEEPNORM_ALPHA = (2.0 * DEPTH) ** 0.25
IDX_SCALE = HEADS ** -0.5 * HEAD_DIM ** -0.5
QK_SCALE = HEAD_DIM ** -0.5
LOG2E = math.log2(math.e)
QK_SCALE_LOG2 = QK_SCALE * LOG2E

SUBLANES = 8
LANES = 128
QB = 256
KC = 256
PAIRS = HEADS // 2
PROJ_CHUNKS = 4
MERGE_CHUNKS = 4
MERGE_ROWS = 256
DEN_ROWS = 16
NEG = -1e30
SB_DONE_BITS = 160.0
INT_MIN = -2 ** 31
VMEM_LIMIT = 56 * 1024 * 1024
MXU_DTYPE = jnp.bfloat16
IDX_DTYPE = jnp.float32

_COLS = [WIDTH, WIDTH, WIDTH, WIDTH, WIDTH, HEAD_DIM, HEAD_DIM, WIDTH, WIDTH, HEAD_DIM, HEADS,
         D_MODEL, D_MODEL]
_OFF = [0]
for _c in _COLS:
    _OFF.append(_OFF[-1] + _c)
(O_QA, O_KA, O_VA, O_ZA, O_QB, O_KB, O_VB, O_ZB, O_QI, O_KI, O_WI, O_GA, O_GB, O_END) = _OFF


def _layer_norm(x, g, b):
    mu = jnp.mean(x, axis=-1, keepdims=True)
    xc = x - mu
    var = jnp.mean(xc * xc, axis=-1, keepdims=True)
    return xc * lax.rsqrt(var + LN_EPS) * g + b


def _dot(a, b):
    return jnp.dot(a, b, preferred_element_type=jnp.float32)


def _dot_nt(a, b):
    return lax.dot_general(a, b, (((1,), (1,)), ((), ())), preferred_element_type=jnp.float32)


WQ_COLS = 2 * WIDTH
WQI_COLS = 2 * WIDTH
WK_COLS = WIDTH + LANES
WV_ROWS = WIDTH + HEAD_DIM


def _proj_kernel(x_ref, g_ref, b_ref, perm_ref, wq_ref, wqi_ref, wk_ref, wki_ref, wvt_ref, wit_ref,
                 kng_ref, knb_ref,
                 qa_ref, qb_ref, qi_ref, wi_ref, ka_ref, ki_ref, kb_ref, vat_ref, vbt_ref):
    for c in range(x_ref.shape[1] // KC):
        rows = slice(c * KC, (c + 1) * KC)
        hi = _layer_norm(x_ref[0, rows], g_ref[...], b_ref[...]).astype(IDX_DTYPE)
        h = hi.astype(MXU_DTYPE)
        hp = _dot(perm_ref[...].astype(MXU_DTYPE), h).astype(MXU_DTYPE)
        hpi = _dot(perm_ref[...], hi).astype(IDX_DTYPE)

        def store_pairs(ref, val):
            for pr in range(PAIRS):
                ref[0, pr, rows] = val[:, pr * LANES:(pr + 1) * LANES].astype(ref.dtype)

        store_pairs(qa_ref, _dot(h, wq_ref[:, 0:WIDTH]) * QK_SCALE_LOG2)
        store_pairs(qb_ref, _dot(h, wq_ref[:, WIDTH:WQ_COLS]) * QK_SCALE_LOG2)
        store_pairs(qi_ref, _dot(hi, wqi_ref[:, 0:WIDTH]) * _dot(hi, wqi_ref[:, WIDTH:WQI_COLS]) * IDX_SCALE)
        wi_ref[0, :, rows] = _dot_nt(wit_ref[...], hi)

        store_pairs(ka_ref, _dot(hp, wk_ref[:, 0:WIDTH]))
        kb_ref[0, rows] = _dot(hp, wk_ref[:, WIDTH:WK_COLS]).astype(kb_ref.dtype)
        ki_ref[0, rows] = _layer_norm(_dot(hpi, wki_ref[...]), kng_ref[...], knb_ref[...]).astype(ki_ref.dtype)
        vat = _dot_nt(wvt_ref[0:WIDTH, :], hp)
        vat_ref[0, :, c] = vat.reshape(HEADS, HEAD_DIM, KC).astype(vat_ref.dtype)
        vbt_ref[0, c] = _dot_nt(wvt_ref[WIDTH:WV_ROWS, :], hp).astype(vbt_ref.dtype)


def _proj_call(x, ln_g, ln_b, perm, wq, wqi, wk, wki, wvt, wit, kn_g2, kn_b2, step_chunks):
    bsz, s, d = x.shape
    rows = step_chunks * KC
    assert s % rows == 0 and perm.shape == (KC, KC)
    nc = s // KC
    bf = MXU_DTYPE
    pair_shape = lambda dt: jax.ShapeDtypeStruct((bsz, PAIRS, s, LANES), dt)
    pair_spec = pl.BlockSpec((1, PAIRS, rows, LANES), lambda b, i: (b, 0, i, 0))
    row128 = pl.BlockSpec((1, rows, LANES), lambda b, i: (b, i, 0))
    row_shape = lambda dt: jax.ShapeDtypeStruct((bsz, s, LANES), dt)
    const = lambda shape: pl.BlockSpec(shape, lambda b, i: (0,) * len(shape))
    return pl.pallas_call(
        _proj_kernel,
        grid=(bsz, s // rows),
        in_specs=[pl.BlockSpec((1, rows, d), lambda b, i: (b, i, 0)),
                  const((1, d)), const((1, d)), const((KC, KC)),
                  const((d, WQ_COLS)), const((d, WQI_COLS)), const((d, WK_COLS)), const((d, LANES)),
                  const((WV_ROWS, d)), const((HEADS, d)),
                  const((1, LANES)), const((1, LANES))],
        out_specs=[pair_spec, pair_spec, pair_spec,
                   pl.BlockSpec((1, HEADS, rows), lambda b, i: (b, 0, i)),
                   pair_spec, row128, row128,
                   pl.BlockSpec((1, HEADS, rows // KC, HEAD_DIM, KC), lambda b, i: (b, 0, i, 0, 0)),
                   pl.BlockSpec((1, rows // KC, HEAD_DIM, KC), lambda b, i: (b, i, 0, 0))],
        out_shape=[pair_shape(bf), pair_shape(bf), pair_shape(IDX_DTYPE),
                   jax.ShapeDtypeStruct((bsz, HEADS, s), jnp.float32),
                   pair_shape(bf), row_shape(IDX_DTYPE), row_shape(bf),
                   jax.ShapeDtypeStruct((bsz, HEADS, nc, HEAD_DIM, KC), bf),
                   jax.ShapeDtypeStruct((bsz, nc, HEAD_DIM, KC), bf)],
        compiler_params=pltpu.CompilerParams(
            dimension_semantics=("arbitrary", "arbitrary"), vmem_limit_bytes=VMEM_LIMIT),
        name="proj",
    )(x, ln_g, ln_b, perm, wq, wqi, wk, wki, wvt, wit, kn_g2, kn_b2)


def _group_excl_scan(g_tot, reverse):
    sub = lax.broadcasted_iota(jnp.int32, g_tot.shape, 0)
    out = jnp.zeros_like(g_tot)
    for g in range(SUBLANES):
        row = g_tot[g:g + 1, :]
        take = (sub < g) if reverse else (sub > g)
        out = out + jnp.where(take, row, 0.0)
    return out


def _bit_planes(key):
    assert key.shape[0] == 32 * SUBLANES
    k3 = key.reshape(32, SUBLANES, key.shape[1])
    a = [k3[j] ^ jnp.int32(INT_MIN) for j in range(32)]
    as_i32 = lambda m: jnp.int32(m - (1 << 32) if m >= (1 << 31) else m)
    j, m = 16, 0x0000FFFF
    while j:
        shift = jnp.full(a[0].shape, j, jnp.int32)
        k = 0
        while k < 32:
            t = (a[k] ^ lax.shift_right_logical(a[k + j], shift)) & as_i32(m)
            a[k] = a[k] ^ t
            a[k + j] = a[k + j] ^ lax.shift_left(t, shift)
            k = (k + j + 1) & ~j
        j >>= 1
        m = (m ^ (m << j)) & 0xFFFFFFFF
    return a


def _attn_kernel(qa_ref, qb_ref, qi_ref, wi_ref,
                 ka_ref, vat_ref, ki_ref, kb_ref, vbt_ref,
                 kam_ref, vatm_ref, kim_ref, kbm_ref, vbtm_ref,
                 biasd_ref, biasp_ref, biasm_ref,
                 ya_ref, yb_ref,
                 qpad_scr, qipad_scr, key_scr, mb_scr, sbacc_scr, sbcar_scr, dacc_scr, m_scr, z_scr,
                 plane_scr, alive_scr, ones_scr, *, seq, topk):
    i = pl.program_id(1)
    nslab = KC // SUBLANES
    mslab = N_META // SUBLANES
    f32 = jnp.float32

    lane = lax.broadcasted_iota(jnp.int32, (QB, LANES), 1)
    lo_half = (lane < HEAD_DIM).astype(f32)
    hi_half = (lane >= HEAD_DIM).astype(f32)
    for pr in range(PAIRS):
        for t, ref in enumerate((qa_ref, qb_ref)):
            qp = ref[0, pr]
            qpad_scr[t, 2 * pr] = qp * lo_half.astype(qp.dtype)
            qpad_scr[t, 2 * pr + 1] = qp * hi_half.astype(qp.dtype)
        qp = qi_ref[0, pr]
        qipad_scr[2 * pr] = qp * lo_half.astype(qp.dtype)
        qipad_scr[2 * pr + 1] = qp * hi_half.astype(qp.dtype)

    r_io = lax.broadcasted_iota(jnp.int32, (KC, QB), 0)
    c_io = lax.broadcasted_iota(jnp.int32, (KC, QB), 1)
    pos = (r_io & (SUBLANES - 1)) * nslab + (r_io >> 3)
    mask_lt = pos < c_io
    mask_le = pos <= c_io

    diag0 = pl.multiple_of(i * KC, KC)

    def issue(*streams):
        for logits_of_head, _, rows, branch in streams:
            for h in range(HEADS):
                z_scr[branch, h, 0:rows] = logits_of_head(h)

    def finish(*streams):
        for h in range(HEADS):
            for _, rest_of_head, rows, branch in streams:
                rest_of_head(h, z_scr[branch, h, 0:rows])()

    def sb_chunk(k_of_pair, vt_of_head, mask, n_j, pad_rows, first):
        def logits(h):
            return _dot_nt(k_of_pair(h // 2), qpad_scr[0, h])

        def rest(h, z):
            sp = jnp.maximum(z, 0.0) + jnp.log2(1.0 + jnp.exp2(-jnp.abs(z)))
            if mask is not None:
                sp = jnp.where(mask, sp, 0.0)
            sp3 = sp.reshape(n_j, SUBLANES, QB)
            z3 = z.reshape(n_j, SUBLANES, QB)
            run = jnp.zeros((SUBLANES, QB), f32)
            u = [None] * n_j
            for j in reversed(range(n_j)):
                run = run + sp3[j]
                u[j] = jnp.exp2(z3[j] - run)
            carry = jnp.zeros((SUBLANES, QB), f32) if first else sbcar_scr[h]
            scale = jnp.exp2(-(_group_excl_scan(run, reverse=True) + carry))
            a = jnp.concatenate([u[j] * scale for j in range(n_j)], axis=0)
            if mask is not None:
                a = jnp.where(mask, a, 0.0)
            a = a.astype(MXU_DTYPE)
            if pad_rows:
                a = jnp.concatenate([a, jnp.zeros((pad_rows, QB), MXU_DTYPE)], axis=0)
            sbcar_scr[h] = carry + jnp.sum(run, axis=0, keepdims=True)

            def finish():
                pv = _dot(vt_of_head(h), a)
                sbacc_scr[h] = pv if first else sbacc_scr[h] + pv

            return finish

        return logits, rest, n_j * SUBLANES, 0

    def sb_real_chunk(c, mask, first):
        r0 = pl.multiple_of(c * KC, KC)
        return sb_chunk(lambda pr: ka_ref[0, pr, pl.ds(r0, KC), :], lambda h: vat_ref[0, h, c],
                        mask, nslab, 0, first)

    w = wi_ref[0]
    lo = jnp.where(w >= 0.0, 0.0, -jnp.inf)
    hi = jnp.where(w >= 0.0, jnp.inf, 0.0)

    def score_keys(kic, mask):
        acc = jnp.zeros((kic.shape[0], QB), f32)
        for h in range(HEADS):
            s = _dot_nt(kic, qipad_scr[h])
            acc = acc + jnp.minimum(jnp.maximum(s, lo[h:h + 1, :]), hi[h:h + 1, :])
        if mask is not None:
            acc = jnp.where(mask, acc, -jnp.inf)
        bits = lax.bitcast_convert_type(acc, jnp.int32)
        key = jnp.where(bits < 0, bits ^ jnp.int32(0x7FFFFFFF), bits)
        return jnp.where(bits == jnp.int32(INT_MIN), 0, key)

    nchunks = seq // KC

    @pl.when(i == 0)
    def _():
        for c in range(1, nchunks):
            plane_scr[c] = jnp.zeros((32, SUBLANES, QB), jnp.int32)

    def store_chunk_keys(c, r0, mask):
        key = score_keys(ki_ref[0, pl.ds(r0, KC), :], mask)
        key_scr[pl.ds(r0, KC), :] = key
        planes = _bit_planes(key)
        for o in range(32):
            plane_scr[c, o] = planes[o]

    def score_body(c, _):
        store_chunk_keys(c, pl.multiple_of(c * KC, KC), None)
        return 0

    lax.fori_loop(0, i, score_body, 0)
    store_chunk_keys(i, diag0, mask_le)
    meta_keys = score_keys(kim_ref[...], None)
    key_scr[seq:seq + N_META, :] = meta_keys

    def count_meta(cand, strict):
        cb = jnp.broadcast_to(cand, (N_META, QB))
        hit = (meta_keys > cb) if strict else (meta_keys >= cb)
        return jnp.sum(jnp.where(hit, 1.0, 0.0), axis=0, keepdims=True)

    def radix_step(it, st):
        ans, above, took = st
        cand = ans ^ lax.shift_left(jnp.int32(1), 31 - it)
        took_b = jnp.broadcast_to(took, (SUBLANES, QB)) != 0
        acc = jnp.zeros((SUBLANES, QB), jnp.int32)
        for c in range(nchunks):
            ones = ones_scr[c]
            alive = jnp.where(took_b, ones, alive_scr[c] ^ ones)
            ones = alive & plane_scr[c, it]
            alive_scr[c] = alive
            ones_scr[c] = ones
            acc = acc + lax.population_count(ones)
        ones_cnt = jnp.sum(acc.astype(f32), axis=0, keepdims=True)
        take = (above + ones_cnt + count_meta(cand, False)) >= float(topk)
        return (jnp.where(take, cand, ans), jnp.where(take, above, above + ones_cnt),
                jnp.where(take, -1, 0).astype(jnp.int32))

    for c in range(nchunks):
        alive_scr[c] = jnp.full((SUBLANES, QB), -1, jnp.int32)
        ones_scr[c] = jnp.full((SUBLANES, QB), -1, jnp.int32)

    thr, above, _ = lax.fori_loop(
        0, 32, radix_step,
        (jnp.full((1, QB), INT_MIN, jnp.int32), jnp.zeros((1, QB), f32), jnp.full((1, QB), -1, jnp.int32)))

    need = float(topk) - (above + count_meta(thr, True))
    thr_b = jnp.broadcast_to(thr, (SUBLANES, QB))
    need_b = jnp.broadcast_to(need, (SUBLANES, QB))

    def select_rows(blk, n_j, mask, carry):
        b3 = blk.reshape(n_j, SUBLANES, QB)
        tie = [jnp.where(b3[j] == thr_b, 1.0, 0.0) for j in range(n_j)]
        run = jnp.zeros((SUBLANES, QB), f32)
        prefix = []
        for j in range(n_j):
            run = run + tie[j]
            prefix.append(run)
        off = _group_excl_scan(run, reverse=False) + carry
        rows = []
        for j in range(n_j):
            rank = prefix[j] - tie[j] + off
            sel = (b3[j] > thr_b) | ((tie[j] > 0.0) & (rank < need_b))
            rows.append(jnp.where(sel, 0.0, NEG))
        mb = jnp.concatenate(rows, axis=0)
        if mask is not None:
            mb = jnp.where(mask, mb, NEG)
        return mb, carry + jnp.sum(run, axis=0, keepdims=True)

    mb, tie_carry = select_rows(key_scr[seq:seq + N_META, :], mslab, None, jnp.zeros((1, QB), f32))
    mb_scr[seq:seq + N_META, :] = mb

    def select_body(c, carry):
        r0 = pl.multiple_of(c * KC, KC)
        mb_c, carry = select_rows(key_scr[pl.ds(r0, KC), :], nslab, None, carry)
        mb_scr[pl.ds(r0, KC), :] = mb_c
        return carry

    tie_carry = lax.fori_loop(0, i, select_body, tie_carry)
    mb, _ = select_rows(key_scr[pl.ds(diag0, KC), :], nslab, mask_le, tie_carry)
    mb_scr[pl.ds(diag0, KC), :] = mb

    def dsa_chunk(kc_fn, vt_fn, mbias_fn, bias_of_head, rows, pad_rows, first, last, buf=1):
        def logits(h):
            return _dot_nt(kc_fn(), qpad_scr[1, h])

        def rest(h, qk):
            lg = qk + mbias_fn()
            if bias_of_head is not None:
                lg = lg + bias_of_head(h)
            cmax = jnp.max(lg, axis=0, keepdims=True)
            if first:
                m_new = jnp.broadcast_to(cmax, (SUBLANES, QB))
            else:
                m_old = m_scr[h]
                m_new = jnp.maximum(m_old, cmax)
                alpha = jnp.exp2(m_old - m_new)
            p3 = lg.reshape(lg.shape[0] // SUBLANES, SUBLANES, QB)
            pb = jnp.exp2(p3 - m_new[None]).reshape(lg.shape).astype(MXU_DTYPE)
            if pad_rows:
                pb = jnp.concatenate([pb, jnp.zeros((pad_rows, QB), MXU_DTYPE)], axis=0)
            if not last:
                m_scr[h] = m_new

            def finish():
                vt = vt_fn()
                vt_ones = jnp.concatenate([vt, jnp.ones((DEN_ROWS, vt.shape[1]), vt.dtype)], axis=0)
                pv = _dot(vt_ones, pb)
                acc = pv if first else dacc_scr[h] * alpha[0:1, :] + pv
                if last:
                    yb_ref[0, h] = (acc[0:HEAD_DIM] / acc[HEAD_DIM:HEAD_DIM + 1]).astype(yb_ref.dtype)
                else:
                    dacc_scr[h] = acc

            return finish

        return logits, rest, rows, buf

    def dsa_real_chunk(c, bias_of_head, first, buf=1):
        r0 = pl.multiple_of(c * KC, KC)
        return dsa_chunk(lambda: kb_ref[0, pl.ds(r0, KC), :], lambda: vbt_ref[0, c],
                         lambda: mb_scr[pl.ds(r0, KC), :], bias_of_head, KC, 0, first, False, buf)

    def run_pair(*streams):
        issue(*streams)
        finish(*streams)

    def pair(p, bias_of_head=None):
        return sb_real_chunk(i - 1 - p, None, False), dsa_real_chunk(p, bias_of_head, False)

    def sb_open():
        return jnp.min(jnp.min(sbcar_scr[...], axis=0)) < SB_DONE_BITS

    run_pair(sb_real_chunk(i, mask_lt, True), dsa_real_chunk(i, lambda h: biasd_ref[h], True))

    def both_cond(st):
        p, still_open = st
        return (p < i - 1) & still_open

    def both_body(st):
        run_pair(*pair(st[0]))
        return st[0] + 1, sb_open()

    p_sparse, _ = lax.while_loop(both_cond, both_body, (jnp.int32(0), sb_open()))

    n_sparse = jnp.maximum(i - 1 - p_sparse, 0)

    def sparse_two(k, _):
        p = p_sparse + 2 * k
        run_pair(dsa_real_chunk(p, None, False), dsa_real_chunk(p + 1, None, False, buf=0))
        return 0

    lax.fori_loop(0, lax.shift_right_logical(n_sparse, 1), sparse_two, 0)

    @pl.when((n_sparse & 1) == 1)
    def _():
        run_pair(dsa_real_chunk(i - 2, None, False))

    prev_bias = lambda h: biasp_ref[h]
    open_prev = sb_open()

    @pl.when((i > 0) & open_prev)
    def _():
        run_pair(*pair(i - 1, prev_bias))

    first_block = (i == 0).astype(f32)
    sb_meta = sb_chunk(lambda pr: kam_ref[pr], lambda h: vatm_ref[h], None, mslab, LANES - N_META, False)

    def dsa_meta(buf):
        return dsa_chunk(lambda: kbm_ref[...], lambda: vbtm_ref[...], lambda: mb_scr[seq:seq + N_META, :],
                         lambda h: biasm_ref[h] * first_block, N_META, LANES - N_META, False, True, buf)

    closed_prev = (i > 0) & jnp.logical_not(open_prev)

    @pl.when(closed_prev)
    def _():
        run_pair(dsa_real_chunk(i - 1, prev_bias, False), dsa_meta(0))

    open_meta = sb_open()

    @pl.when(jnp.logical_not(closed_prev) & open_meta)
    def _():
        run_pair(sb_meta, dsa_meta(1))

    @pl.when(jnp.logical_not(closed_prev) & jnp.logical_not(open_meta))
    def _():
        run_pair(dsa_meta(1))

    for h in range(HEADS):
        ya_ref[0, h] = sbacc_scr[h].astype(ya_ref.dtype)


def _attn_call(qa, qb, qi, wit, ka, vat, ki, kb, vbt, kam, vatm, kim, kbm, vbtm,
               bias_d, bias_p, bias_m, topk):
    bsz, _, s, _ = qa.shape
    nq = s // QB
    nc = s // KC
    bf = MXU_DTYPE
    qspec = pl.BlockSpec((1, PAIRS, QB, LANES), lambda b, i: (b, 0, i, 0))
    const = lambda shape: pl.BlockSpec(shape, lambda b, i: (0,) * len(shape))
    yspec = pl.BlockSpec((1, HEADS, HEAD_DIM, QB), lambda b, i: (b, 0, 0, i))
    yshape = jax.ShapeDtypeStruct((bsz, HEADS, HEAD_DIM, s), bf)
    return pl.pallas_call(
        functools.partial(_attn_kernel, seq=s, topk=topk),
        grid=(bsz, nq),
        in_specs=[qspec, qspec, qspec,
                  pl.BlockSpec((1, HEADS, QB), lambda b, i: (b, 0, i)),
                  pl.BlockSpec((1, PAIRS, s, LANES), lambda b, i: (b, 0, 0, 0)),
                  pl.BlockSpec((1, HEADS, nc, HEAD_DIM, KC), lambda b, i: (b, 0, 0, 0, 0)),
                  pl.BlockSpec((1, s, LANES), lambda b, i: (b, 0, 0)),
                  pl.BlockSpec((1, s, LANES), lambda b, i: (b, 0, 0)),
                  pl.BlockSpec((1, nc, HEAD_DIM, KC), lambda b, i: (b, 0, 0, 0)),
                  const((PAIRS, N_META, LANES)), const((HEADS, HEAD_DIM, LANES)),
                  const((N_META, LANES)), const((N_META, LANES)), const((HEAD_DIM, LANES)),
                  const((HEADS, KC, QB)), const((HEADS, KC, QB)), const((HEADS, N_META, QB))],
        out_specs=[yspec, yspec],
        out_shape=[yshape, yshape],
        scratch_shapes=[pltpu.VMEM((2, HEADS, QB, LANES), bf),
                        pltpu.VMEM((HEADS, QB, LANES), IDX_DTYPE),
                        pltpu.VMEM((s + N_META, QB), jnp.int32),
                        pltpu.VMEM((s + N_META, QB), jnp.float32),
                        pltpu.VMEM((HEADS, HEAD_DIM, QB), jnp.float32),
                        pltpu.VMEM((HEADS, SUBLANES, QB), jnp.float32),
                        pltpu.VMEM((HEADS, HEAD_DIM + DEN_ROWS, QB), jnp.float32),
                        pltpu.VMEM((HEADS, SUBLANES, QB), jnp.float32),
                        pltpu.VMEM((2, HEADS, KC, QB), jnp.float32),
                        pltpu.VMEM((nc, 32, SUBLANES, QB), jnp.int32),
                        pltpu.VMEM((nc, SUBLANES, QB), jnp.int32),
                        pltpu.VMEM((nc, SUBLANES, QB), jnp.int32)],
        compiler_params=pltpu.CompilerParams(
            dimension_semantics=("arbitrary", "arbitrary"), vmem_limit_bytes=VMEM_LIMIT),
        name="attn",
    )(qa, qb, qi, wit, ka, vat, ki, kb, vbt, kam, vatm, kim, kbm, vbtm, bias_d, bias_p, bias_m)


def _sigmoid(x):
    return 1.0 / (1.0 + jnp.exp(-x))


def _merge_kernel(x_ref, ya_ref, yb_ref, g_in_ref, b_in_ref, wz_ref, bg_ref, wpa_ref, wpb_ref,
                  wo_ref, g_ref, b_ref, o_ref):
    for t in range(x_ref.shape[1] // MERGE_ROWS):
        rows = slice(t * MERGE_ROWS, (t + 1) * MERGE_ROWS)
        h = _layer_norm(x_ref[0, rows], g_in_ref[...], b_in_ref[...])
        hb = h.astype(MXU_DTYPE)

        def branch(yt_ref, z_col, g_col, bg_col, wp_ref):
            z = _dot(hb, wz_ref[:, z_col:z_col + WIDTH])
            y = yt_ref[0, :, rows].astype(jnp.float32).T
            y = (y * (z * _sigmoid(z))).astype(MXU_DTYPE)
            gate = _sigmoid(_dot(hb, wz_ref[:, g_col:g_col + D_MODEL]) + bg_ref[:, bg_col:bg_col + D_MODEL])
            return gate * _dot(y, wp_ref[...])

        merged = (branch(ya_ref, 0, 2 * WIDTH, 0, wpa_ref)
                  + branch(yb_ref, WIDTH, 2 * WIDTH + D_MODEL, D_MODEL, wpb_ref))
        out = _dot(merged.astype(MXU_DTYPE), wo_ref[...])
        o_ref[0, rows] = _layer_norm(DEEPNORM_ALPHA * h + out, g_ref[...], b_ref[...])


def _merge_call(x, ya, yb, ln_in_g, ln_in_b, wz, bg, wpa, wpb, wo, ln_g, ln_b, tm):
    bsz, s, d = x.shape
    assert s % tm == 0
    const = lambda shape: pl.BlockSpec(shape, lambda b, i: (0,) * len(shape))
    row = lambda width: pl.BlockSpec((1, tm, width), lambda b, i: (b, i, 0))
    col = pl.BlockSpec((1, WIDTH, tm), lambda b, i: (b, 0, i))
    return pl.pallas_call(
        _merge_kernel,
        grid=(bsz, s // tm),
        in_specs=[row(d), col, col, const((1, d)), const((1, d)),
                  const((d, 2 * WIDTH + 2 * d)), const((1, 2 * d)),
                  const((WIDTH, d)), const((WIDTH, d)), const((d, d)), const((1, d)), const((1, d))],
        out_specs=row(d),
        out_shape=jax.ShapeDtypeStruct((bsz, s, d), jnp.float32),
        compiler_params=pltpu.CompilerParams(
            dimension_semantics=("arbitrary", "arbitrary"), vmem_limit_bytes=VMEM_LIMIT),
        name="merge",
    )(x, ya, yb, ln_in_g, ln_in_b, wz, bg, wpa, wpb, wo, ln_g, ln_b)


def _perm_rows(a, axis, chunk):
    shp = a.shape
    n = shp[axis] // chunk
    a = a.reshape(shp[:axis] + (n, SUBLANES, chunk // SUBLANES) + shp[axis + 1:])
    a = jnp.swapaxes(a, axis + 1, axis + 2)
    return a.reshape(shp)


def _rel_bucket(dist):
    max_exact = REL_BUCKETS // 2
    nf = np.maximum(dist, 1).astype(np.float32)
    large = max_exact + (np.log(nf / np.float32(max_exact)) / np.float32(math.log(REL_MAX_DIST / max_exact))
                         * np.float32(REL_BUCKETS - max_exact)).astype(np.int32)
    large = np.minimum(large, REL_BUCKETS - 1)
    return np.where(dist < max_exact, dist, large)


def _bias_tiles(rel_bias):
    r = np.arange(KC)
    pos = (r % SUBLANES) * (KC // SUBLANES) + r // SUBLANES
    q = np.arange(QB)
    rm = np.arange(N_META)
    posm = (rm % SUBLANES) * (N_META // SUBLANES) + rm // SUBLANES
    table = (rel_bias - rel_bias[REL_BUCKETS - 1]).astype(jnp.float32) * LOG2E

    def tile(dist):
        onehot = jnp.asarray(_rel_bucket(dist)[..., None] == np.arange(REL_BUCKETS), jnp.float32)
        return jnp.einsum("rqb,bh->hrq", onehot, table, precision=lax.Precision.HIGHEST)

    return (tile(np.maximum(q[None, :] - pos[:, None], 0)),
            tile(KC + q[None, :] - pos[:, None]),
            tile(N_META + q[None, :] - posm[:, None]))


def kernel(x, meta_tokens, ln_in_g, ln_in_b, rel_bias, w_in, b_gate, idx_kn_g, idx_kn_b,
           w_pa, w_pb, w_o, ln_g, ln_b):
    bsz, seq, d = x.shape
    assert d == D_MODEL and seq % KC == 0 and w_in.shape[0] == DEPTH == 1
    topk = min(TOPK_MAX, seq // 4)
    nc = seq // KC
    bf = MXU_DTYPE
    w = w_in[0]
    dup = lambda a: jnp.concatenate([a, a], axis=-1)
    cols = lambda off, width: w[:, off:off + width]
    w_q = jnp.concatenate([cols(O_QA, WIDTH), cols(O_QB, WIDTH)], axis=1).astype(bf)
    w_qi = jnp.concatenate([cols(O_QI, WIDTH), jnp.repeat(cols(O_WI, HEADS), HEAD_DIM, axis=1)],
                           axis=1).astype(IDX_DTYPE)
    w_k = jnp.concatenate([cols(O_KA, WIDTH), dup(cols(O_KB, HEAD_DIM))], axis=1).astype(bf)
    w_ki = dup(cols(O_KI, HEAD_DIM)).astype(IDX_DTYPE)
    w_vt = jnp.concatenate([cols(O_VA, WIDTH), cols(O_VB, HEAD_DIM)], axis=1).T.astype(bf)
    w_wit = cols(O_WI, HEADS).T.astype(IDX_DTYPE)
    w_z = jnp.concatenate([cols(O_ZA, WIDTH), cols(O_ZB, WIDTH), cols(O_GA, D_MODEL), cols(O_GB, D_MODEL)],
                          axis=1).astype(bf)
    row = lambda a: a.reshape(1, -1).astype(jnp.float32)
    r = np.arange(KC)
    perm = np.zeros((KC, KC), np.float32)
    perm[r, (r % SUBLANES) * (KC // SUBLANES) + r // SUBLANES] = 1.0

    proj = functools.partial(_proj_call, ln_g=row(ln_in_g), ln_b=row(ln_in_b), wq=w_q, wqi=w_qi, wk=w_k,
                             wki=w_ki, wvt=w_vt, wit=w_wit,
                             kn_g2=row(dup(idx_kn_g[0])), kn_b2=row(dup(idx_kn_b[0])))
    qa, qb, qi, wit, ka, ki, kb, vat, vbt = proj(x, perm=jnp.asarray(perm, IDX_DTYPE),
                                                 step_chunks=math.gcd(nc, PROJ_CHUNKS))
    x_meta = jnp.pad(meta_tokens.astype(x.dtype), ((0, KC - N_META), (0, 0)))[None]
    _, _, _, _, kam, kim, kbm, vatm, vbtm = proj(x_meta, perm=jnp.eye(KC, dtype=IDX_DTYPE), step_chunks=1)
    pad_keys = lambda a: jnp.pad(a, [(0, 0)] * (a.ndim - 1) + [(0, LANES - N_META)])
    kam = _perm_rows(kam[0, :, :N_META], 1, N_META)
    kim = _perm_rows(kim[0, :N_META], 0, N_META)
    kbm = _perm_rows(kbm[0, :N_META], 0, N_META)
    vatm = pad_keys(_perm_rows(vatm[0, :, 0, :, :N_META], 2, N_META))
    vbtm = pad_keys(_perm_rows(vbtm[0, 0, :, :N_META], 1, N_META))
    bias_d, bias_p, bias_m = _bias_tiles(rel_bias)

    yat, ybt = _attn_call(qa, qb, qi, wit, ka, vat, ki, kb, vbt, kam, vatm, kim, kbm, vbtm,
                          bias_d, bias_p, bias_m, topk)
    ya = yat.reshape(bsz, WIDTH, seq)
    yb = ybt.reshape(bsz, WIDTH, seq)

    return _merge_call(x, ya, yb, row(ln_in_g), row(ln_in_b), w_z, row(b_gate[0]),
                       w_pa[0].astype(bf), w_pb[0].astype(bf), w_o[0].astype(bf),
                       row(ln_g[0]), row(ln_b[0]), tm=KC * math.gcd(nc, MERGE_CHUNKS))
```
